```python
import math
import jax
import jax.numpy as jnp
from jax import lax
import numpy as np

D_MODEL = 1024
BATCH = 16
SEQ = 4096
DEPTH = 4

CHUNK = 64
BLOCK_Q = 128
N_BRANCH = 4
HEAD_DIM = 64
BRANCH_HEADS = 4
BRANCH_WIDTH = BRANCH_HEADS * HEAD_DIM

LRU_WIDTH = BRANCH_WIDTH
LRU_BLOCKS = BRANCH_HEADS
LRU_BLOCK = LRU_WIDTH // LRU_BLOCKS
LRU_C = 8.0
CONV_W = 4
SB_HEADS = BRANCH_HEADS
SB_WIDTH = BRANCH_WIDTH
GDN_HEADS = BRANCH_HEADS
GDN_WIDTH = BRANCH_WIDTH
RW_HEADS = BRANCH_HEADS
RW_WIDTH = BRANCH_WIDTH
W_LORA = 32
A_LORA = 32
G_LORA = 64
RW_LN_EPS = 64e-5
FFN_DIM = 2816
FFN_CONV_W = 3
EPS = 1e-6

LRU_IN = (LRU_WIDTH, LRU_WIDTH)
SB_IN = (SB_WIDTH, SB_WIDTH, SB_WIDTH)
GDN_IN = (GDN_WIDTH, GDN_WIDTH, GDN_WIDTH, GDN_WIDTH, GDN_HEADS, GDN_HEADS)
RW_IN = (RW_WIDTH, RW_WIDTH, RW_WIDTH, W_LORA, A_LORA, G_LORA)
MIXER_IN = (sum(LRU_IN), sum(SB_IN), sum(GDN_IN), sum(RW_IN))
IN_DIM = sum(MIXER_IN)

kernel_name = "hybrid_gated_parallel_mixer_trunk"


def _split(t, widths):
    cuts = [int(v) for v in np.cumsum(widths)[:-1]]
    return jnp.split(t, cuts, axis=-1)


def rms_norm(x, g):
    xf = x.astype(jnp.float32)
    y = xf * lax.rsqrt(jnp.mean(xf * xf, axis=-1, keepdims=True) + EPS)
    return (y * g.astype(jnp.float32)).astype(x.dtype)


def l2_normalize(t):
    return t * lax.rsqrt(jnp.sum(t * t, axis=-1, keepdims=True) + EPS)


def causal_depthwise_conv(x, w):
    width, seq = w.shape[0], x.shape[1]
    xp = jnp.pad(x, ((0, 0), (width - 1, 0), (0, 0)))
    return sum(xp[:, j:j + seq] * w[j] for j in range(width))


def token_shift(x):
    return jnp.pad(x, ((0, 0), (1, 0), (0, 0)))[:, :-1]


def rg_lru_branch(p, conv_w, conv_b, w_r, b_r, w_i, b_i, lam):
    bsz, seq, _ = p.shape
    x_in, y_in = _split(p, LRU_IN)
    u = (causal_depthwise_conv(x_in, conv_w) + conv_b).astype(jnp.float32)
    ub = u.reshape(bsz, seq, LRU_BLOCKS, LRU_BLOCK)
    r = jax.nn.sigmoid(jnp.einsum('bsne,nef->bsnf', ub, w_r.astype(jnp.float32)).reshape(bsz, seq, LRU_WIDTH) + b_r)
    i = jax.nn.sigmoid(jnp.einsum('bsne,nef->bsnf', ub, w_i.astype(jnp.float32)).reshape(bsz, seq, LRU_WIDTH) + b_i)
    log_a = -LRU_C * r * jax.nn.softplus(-lam.astype(jnp.float32))
    a = jnp.exp(log_a)
    b = jnp.sqrt(-jnp.expm1(2.0 * log_a)) * (i * u)

    def combine(e1, e2):
        return (e1[0] * e2[0], e2[0] * e1[1] + e2[1])

    _, h = lax.associative_scan(combine, (a, b), axis=1)
    return (h * jax.nn.gelu(y_in.astype(jnp.float32))).astype(p.dtype)


def stick_breaking_branch(p):
    bsz, seq, _ = p.shape
    q, k, v = [t.reshape(bsz, seq, SB_HEADS, HEAD_DIM).astype(jnp.float32) for t in _split(p, SB_IN)]
    scale = HEAD_DIM ** -0.5
    blocks = []
    for blk in range(seq // BLOCK_Q):
        start, end = blk * BLOCK_Q, (blk + 1) * BLOCK_Q
        z = jnp.einsum('bqhd,bshd->bhqs', q[:, start:end], k[:, :end]) * scale
        t_pos = start + jnp.arange(BLOCK_Q)
        s_pos = jnp.arange(end)
        past = s_pos[None, :] < t_pos[:, None]
        neg_log_keep = jnp.where(past, jax.nn.softplus(z), 0.0)
        between = lax.cumsum(neg_log_keep, axis=3, reverse=True) - neg_log_keep
        weight = jnp.where(past, jnp.exp(jax.nn.log_sigmoid(z) - between), 0.0)
        blocks.append(jnp.einsum('bhqs,bshd->bqhd', weight, v[:, :end]))
    return jnp.concatenate(blocks, axis=1).reshape(bsz, seq, SB_WIDTH).astype(p.dtype)


def chunked_gated_delta_rule(q, k, v, g, beta):
    bsz, seq, nh, dk = q.shape
    dv = v.shape[-1]
    nc = seq // CHUNK

    def chunks(t):
        return t.reshape(bsz, nc, CHUNK, nh, t.shape[-1]).transpose(0, 3, 1, 2, 4)

    q, k, v = chunks(q), chunks(k), chunks(v)
    g = g.reshape(bsz, nc, CHUNK, nh).transpose(0, 3, 1, 2)
    beta = beta.reshape(bsz, nc, CHUNK, nh).transpose(0, 3, 1, 2)
    G = jnp.cumsum(g, axis=-1)
    pos = jnp.arange(CHUNK)
    incl = pos[:, None] >= pos[None, :]
    strict = pos[:, None] > pos[None, :]
    decay = jnp.exp(jnp.where(incl, G[..., :, None] - G[..., None, :], -jnp.inf))
    k_beta = k * beta[..., None]
    lower = jnp.where(strict, jnp.einsum('bhnid,bhnjd->bhnij', k_beta, k) * decay, 0.0)
    rhs = jnp.concatenate([v * beta[..., None], k_beta * jnp.exp(G)[..., None]], axis=-1)
    sol = lax.linalg.triangular_solve(lower + jnp.eye(CHUNK, dtype=lower.dtype), rhs,
                                      left_side=True, lower=True, unit_diagonal=True)
    u_intra, w_state = sol[..., :dv], sol[..., dv:]
    qk = jnp.einsum('bhnid,bhnjd->bhnij', q, k) * decay
    q_dec = q * jnp.exp(G)[..., None]
    G_last = G[..., -1:]
    k_dec = k * jnp.exp(G_last - G)[..., None]
    chunk_decay = jnp.exp(G_last[..., 0])

    def step(state, inp):
        u_i, w_i, qk_i, qd_i, kd_i, cd_i = inp
        u = u_i - jnp.einsum('bhik,bhkv->bhiv', w_i, state)
        o = jnp.einsum('bhik,bhkv->bhiv', qd_i, state) + jnp.einsum('bhij,bhjv->bhiv', qk_i, u)
        state = state * cd_i[..., None, None] + jnp.einsum('bhjk,bhjv->bhkv', kd_i, u)
        return state, o

    xs = tuple(jnp.moveaxis(t, 2, 0) for t in (u_intra, w_state, qk, q_dec, k_dec, chunk_decay))
    _, o = lax.scan(step, jnp.zeros((bsz, nh, dk, dv), q.dtype), xs)
    return o.transpose(1, 0, 3, 2, 4).reshape(bsz, seq, nh, dv)


def gated_deltanet_branch(p, conv_w, a_log, dt_bias, norm_g):
    bsz, seq, _ = p.shape
    q, k, v, z, a_in, b_in = _split(p, GDN_IN)
    qkv = jax.nn.silu(causal_depthwise_conv(jnp.concatenate([q, k, v], axis=-1), conv_w)).astype(jnp.float32)

    def heads(t):
        return t.reshape(bsz, seq, GDN_HEADS, HEAD_DIM)

    q, k, v = [heads(t) for t in jnp.split(qkv, 3, axis=-1)]
    q = l2_normalize(q) * (HEAD_DIM ** -0.5)
    k = l2_normalize(k)
    beta = jax.nn.sigmoid(b_in.astype(jnp.float32))
    g = -jnp.exp(a_log.astype(jnp.float32)) * jax.nn.softplus(a_in.astype(jnp.float32) + dt_bias.astype(jnp.float32))
    o = chunked_gated_delta_rule(q, k, v, g, beta)
    o = o * lax.rsqrt(jnp.mean(o * o, axis=-1, keepdims=True) + EPS) * norm_g.astype(jnp.float32)
    o = o * jax.nn.silu(heads(z.astype(jnp.float32)))
    return o.reshape(bsz, seq, GDN_WIDTH).astype(p.dtype)


def rwkv7_scan(r, w, k, v, kk, a):
    bsz, _, nh, hd = r.shape

    def step(state, inp):
        r_t, w_t, k_t, v_t, kk_t, a_t = inp
        sa = jnp.einsum('bhvk,bhk->bhv', state, -kk_t)
        state = (state * w_t[:, :, None, :] + sa[..., None] * (kk_t * a_t)[:, :, None, :]
                 + v_t[..., None] * k_t[:, :, None, :])
        return state, jnp.einsum('bhvk,bhk->bhv', state, r_t)

    xs = tuple(jnp.moveaxis(t, 1, 0) for t in (r, w, k, v, kk, a))
    _, y = lax.scan(step, jnp.zeros((bsz, nh, hd, hd), r.dtype), xs)
    return jnp.moveaxis(y, 0, 1)


def rwkv7_branch(p, mu, w0, w_up, a0, a_up, g_up, k_k, k_a, r_k, ln_g, ln_b):
    bsz, seq, _ = p.shape
    pf = p.astype(jnp.float32)
    pf = pf + (token_shift(pf) - pf) * mu
    r, k, v, xw, xa, xg = _split(pf, RW_IN)
    w_log = -jax.nn.softplus(-(w0 + jnp.tanh(xw) @ w_up)) - 0.5
    decay = jnp.exp(-jnp.exp(w_log))
    a = jax.nn.sigmoid(a0 + xa @ a_up)
    gate = jax.nn.sigmoid(xg) @ g_up

    def heads(t):
        return t.reshape(bsz, seq, RW_HEADS, HEAD_DIM)

    kk = l2_normalize(heads(k * k_k))
    k = k * (1.0 + (a - 1.0) * k_a)
    r_h, k_h, v_h = heads(r), heads(k), heads(v)
    y = rwkv7_scan(r_h, heads(decay), k_h, v_h, kk, heads(a))
    mean = jnp.mean(y, axis=-1, keepdims=True)
    var = jnp.mean(jnp.square(y - mean), axis=-1, keepdims=True)
    y = ((y - mean) * lax.rsqrt(var + RW_LN_EPS)).reshape(bsz, seq, RW_WIDTH) * ln_g + ln_b
    bonus = jnp.sum(r_h * k_h * r_k, axis=-1, keepdims=True) * v_h
    return ((y + bonus.reshape(bsz, seq, RW_WIDTH)) * gate).astype(p.dtype)


def conv_ffn(h, w_gate, w_up, conv_w, w_down):
    a = causal_depthwise_conv(h @ w_gate, conv_w)
    return (jax.nn.gelu(a) * (h @ w_up)) @ w_down


def setup_inputs(seed: int = 0) -> dict:
    key = jax.random.key(seed)
    keys = iter(jax.random.split(key, 64))
    f32 = jnp.float32
    L, D = DEPTH, D_MODEL

    def nrm(shape, scale):
        return jax.random.normal(next(keys), shape, f32) * scale

    def uni(shape, lo, hi):
        return jax.random.uniform(next(keys), shape, f32, lo, hi)

    x = nrm((BATCH, SEQ, D), 1.0)
    c = nrm((BATCH, D), 1.0)
    s = uni((L, LRU_WIDTH), 0.9, 0.999) ** (1.0 / LRU_C)
    lru_lambda = jnp.log(s) - jnp.log1p(-s)
    dt = jnp.exp(uni((L, GDN_HEADS), math.log(1e-3), math.log(1e-1)))
    gdn_dt_bias = dt + jnp.log(-jnp.expm1(-dt))
    return {
        'x': x,
        'c': c,
        'norm1_g': 1.0 + nrm((L, D), 0.02),
        'norm2_g': 1.0 + nrm((L, D), 0.02),
        'final_g': 1.0 + nrm((D,), 0.02),
        'w_ada': nrm((L, D, 6 * D), D ** -0.5),
        'b_ada': nrm((L, 6 * D), 0.02),
        'w_in': nrm((L, D, IN_DIM), D ** -0.5),
        'lru_conv_w': nrm((L, CONV_W, LRU_WIDTH), CONV_W ** -0.5),
        'lru_conv_b': nrm((L, LRU_WIDTH), 0.02),
        'lru_w_r': nrm((L, LRU_BLOCKS, LRU_BLOCK, LRU_BLOCK), LRU_BLOCK ** -0.5),
        'lru_b_r': nrm((L, LRU_WIDTH), 0.02),
        'lru_w_i': nrm((L, LRU_BLOCKS, LRU_BLOCK, LRU_BLOCK), LRU_BLOCK ** -0.5),
        'lru_b_i': nrm((L, LRU_WIDTH), 0.02),
        'lru_lambda': lru_lambda,
        'gdn_conv_w': nrm((L, CONV_W, 3 * GDN_WIDTH), CONV_W ** -0.5),
        'gdn_a_log': jnp.log(uni((L, GDN_HEADS), 1.0, 16.0)),
        'gdn_dt_bias': gdn_dt_bias,
        'gdn_norm_g': 1.0 + nrm((L, HEAD_DIM), 0.02),
        'rw_mu': uni((L, sum(RW_IN)), 0.0, 1.0),
        'rw_w0': uni((L, RW_WIDTH), -6.0, -1.0),
        'rw_w_up': nrm((L, W_LORA, RW_WIDTH), W_LORA ** -0.5),
        'rw_a0': nrm((L, RW_WIDTH), 0.1),
        'rw_a_up': nrm((L, A_LORA, RW_WIDTH), A_LORA ** -0.5),
        'rw_g_up': nrm((L, G_LORA, RW_WIDTH), G_LORA ** -0.5),
        'rw_k_k': 0.85 + nrm((L, RW_WIDTH), 0.05),
        'rw_k_a': 1.0 + nrm((L, RW_WIDTH), 0.05),
        'rw_r_k': nrm((L, RW_HEADS, HEAD_DIM), 0.1),
        'rw_ln_g': 1.0 + nrm((L, RW_WIDTH), 0.02),
        'rw_ln_b': nrm((L, RW_WIDTH), 0.02),
        'w_branch': nrm((L, N_BRANCH, BRANCH_WIDTH, D), BRANCH_WIDTH ** -0.5),
        'w_gate': nrm((L, N_BRANCH, D, D), D ** -0.5),
        'b_gate': nrm((L, N_BRANCH, D), 0.02),
        'w_out': nrm((L, D, D), D ** -0.5),
        'ffn_w_gate': nrm((L, D, FFN_DIM), D ** -0.5),
        'ffn_w_up': nrm((L, D, FFN_DIM), D ** -0.5),
        'ffn_conv_w': nrm((L, FFN_CONV_W, FFN_DIM), FFN_CONV_W ** -0.5),
        'ffn_w_down': nrm((L, FFN_DIM, D), FFN_DIM ** -0.5),
    }


def reference(x, c, norm1_g, norm2_g, final_g, w_ada, b_ada, w_in,
              lru_conv_w, lru_conv_b, lru_w_r, lru_b_r, lru_w_i, lru_b_i, lru_lambda,
              gdn_conv_w, gdn_a_log, gdn_dt_bias, gdn_norm_g,
              rw_mu, rw_w0, rw_w_up, rw_a0, rw_a_up, rw_g_up, rw_k_k, rw_k_a, rw_r_k, rw_ln_g, rw_ln_b,
              w_branch, w_gate, b_gate, w_out,
              ffn_w_gate, ffn_w_up, ffn_conv_w, ffn_w_down):
    cond = jax.nn.silu(c)
    for l in range(DEPTH):
        mod = cond @ w_ada[l] + b_ada[l]
        shift1, scale1, gate1, shift2, scale2, gate2 = [m[:, None, :] for m in jnp.split(mod, 6, axis=-1)]
        h = rms_norm(x, norm1_g[l]) * (1.0 + scale1) + shift1
        p_lru, p_sb, p_gdn, p_rw = _split(h @ w_in[l], MIXER_IN)
        branches = (
            rg_lru_branch(p_lru, lru_conv_w[l], lru_conv_b[l], lru_w_r[l], lru_b_r[l],
                          lru_w_i[l], lru_b_i[l], lru_lambda[l]),
            stick_breaking_branch(p_sb),
            gated_deltanet_branch(p_gdn, gdn_conv_w[l], gdn_a_log[l], gdn_dt_bias[l], gdn_norm_g[l]),
            rwkv7_branch(p_rw, rw_mu[l], rw_w0[l], rw_w_up[l], rw_a0[l], rw_a_up[l], rw_g_up[l],
                         rw_k_k[l], rw_k_a[l], rw_r_k[l], rw_ln_g[l], rw_ln_b[l]),
        )
        mixed = sum(jax.nn.sigmoid(h @ w_gate[l, n] + b_gate[l, n]) * (o @ w_branch[l, n])
                    for n, o in enumerate(branches))
        x = x + gate1 * (mixed @ w_out[l])
        h = rms_norm(x, norm2_g[l]) * (1.0 + scale2) + shift2
        x = x + gate2 * conv_ffn(h, ffn_w_gate[l], ffn_w_up[l], ffn_conv_w[l], ffn_w_down[l])
    return rms_norm(x, final_g)
```

```python
import functools

import jax
import jax.numpy as jnp
from jax import lax
from jax.experimental import pallas as pl
from jax.experimental.pallas import tpu as pltpu

F32 = jnp.float32
MXU_DTYPE = jnp.bfloat16

HEAD_DIM = 64
N_HEADS = 4
WIDTH = N_HEADS * HEAD_DIM
CHUNK = 64
STACK = N_HEADS * CHUNK
SUBLANES = 8
LANES = 128
EPS = 1e-6
RW_LN_EPS = 64e-5
LRU_C = 8.0
W_LORA, A_LORA, G_LORA = 32, 32, 64
FFN_SPLIT = 2

TM_DENSE = 512
TS_LRU = 512
TS_REC = 256
TQ_SB = 256

VMEM_LIMIT = 56 * 1024 * 1024


def _dot(a, b):
    return jnp.dot(a.astype(MXU_DTYPE), b.astype(MXU_DTYPE), preferred_element_type=F32)


def _dot_nt(a, b):
    return lax.dot_general(a.astype(MXU_DTYPE), b.astype(MXU_DTYPE),
                           (((1,), (1,)), ((), ())), preferred_element_type=F32)


def _dot_tn(a, b):
    return lax.dot_general(a.astype(MXU_DTYPE), b.astype(MXU_DTYPE),
                           (((0,), (0,)), ((), ())), preferred_element_type=F32)


def _split(x, parts):
    out = []
    r = x
    for _ in range(parts - 1):
        p = r.astype(jnp.bfloat16)
        out.append(p)
        r = r - p.astype(F32)
    out.append(r.astype(jnp.bfloat16))
    return out


def _dot_x01(x, m01, parts):
    acc = None
    for p in _split(x, parts):
        t = jnp.dot(p, m01, preferred_element_type=F32)
        acc = t if acc is None else acc + t
    return acc


def _dot_01x(m01, x, parts):
    acc = None
    for p in _split(x, parts):
        t = jnp.dot(m01, p, preferred_element_type=F32)
        acc = t if acc is None else acc + t
    return acc


def _dot_hi(a, b):
    ah, al = _split(a, 2)
    bh, bl = _split(b, 2)
    return (jnp.dot(ah, bh, preferred_element_type=F32)
            + (jnp.dot(ah, bl, preferred_element_type=F32)
               + jnp.dot(al, bh, preferred_element_type=F32)))


def _sigmoid(x):
    return jax.nn.sigmoid(x)


def _softplus(x):
    return jnp.maximum(x, 0.0) + jnp.log1p(jnp.exp(-jnp.abs(x)))


def _silu(x):
    return x * _sigmoid(x)


def _gelu_tanh(x):
    return 0.5 * x * (1.0 + jnp.tanh(0.7978845608028654 * (x + 0.044715 * (x * x * x))))


def _iota2(shape, dim):
    return lax.broadcasted_iota(jnp.int32, shape, dim)


def _shift_rows(x, k, halo):
    r = pltpu.roll(x, k, 0)
    hr = pltpu.roll(halo, k, 0)
    first = jnp.where(_iota2(halo.shape, 0) < k, hr, r[0:SUBLANES])
    return jnp.concatenate([first, r[SUBLANES:]], axis=0)


def _shift_fill(x, d, fill):
    n, w = x.shape
    if d % SUBLANES == 0:
        return jnp.concatenate([jnp.full((d, w), fill, x.dtype), x[:n - d]], axis=0)
    r = pltpu.roll(x, d, 0)
    first = jnp.where(_iota2((SUBLANES, w), 0) < d, fill, r[0:SUBLANES])
    return jnp.concatenate([first, r[SUBLANES:]], axis=0)


def _head_ones():
    return ((_iota2((WIDTH, WIDTH), 0) >> 6) == (_iota2((WIDTH, WIDTH), 1) >> 6)).astype(jnp.bfloat16)


def _head_sum(x, ones_bd):
    return _dot_x01(x, ones_bd, 3)


def _stack_masks():
    r = _iota2((STACK, STACK), 0)
    c = _iota2((STACK, STACK), 1)
    same = (r >> 6) == (c >> 6)
    ri = r & (CHUNK - 1)
    ci = c & (CHUNK - 1)
    return dict(
        same=same,
        incl=same & (ri >= ci),
        strict=same & (ri > ci),
        m16=(r >> 4) == (c >> 4),
        m32=(r >> 5) == (c >> 5),
        eye=r == c,
    )


def _stack_bd(x, same):
    return jnp.where(same, jnp.concatenate([x] * N_HEADS, axis=0), 0.0)


def _unstack(y):
    out = y[0:CHUNK]
    for h in range(1, N_HEADS):
        out = out + y[h * CHUNK:(h + 1) * CHUNK]
    return out


def _inv_unit_lower(a, m):
    a0 = jnp.where(m["m16"], a, 0.0)
    a1 = jnp.where(m["m32"] & jnp.logical_not(m["m16"]), a, 0.0)
    a2 = jnp.where(m["m32"], 0.0, a)
    p = jnp.where(m["eye"], 1.0, 0.0) - a0
    x2 = _dot_hi(a0, a0)
    p = p + _dot_hi(p, x2)
    x4 = _dot_hi(x2, x2)
    p = p + _dot_hi(p, x4)
    x8 = _dot_hi(x4, x4)
    p = p + _dot_hi(p, x8)
    p = p - _dot_hi(_dot_hi(p, a1), p)
    p = p - _dot_hi(_dot_hi(p, a2), p)
    return p


def _modulated_norm(x, g, scale, shift):
    ms = jnp.mean(x * x, axis=-1, keepdims=True)
    return (x * lax.rsqrt(ms + EPS) * g) * (1.0 + scale) + shift


def _mod_kernel(c_ref, w_ref, b_ref, o_ref):
    c = c_ref[...]
    cond = c * _sigmoid(c)
    o_ref[...] = jnp.dot(cond, w_ref[...], precision=lax.Precision.HIGHEST,
                         preferred_element_type=F32) + b_ref[...]


def _modulation(c, w_ada, b_ada):
    L, D, D6 = w_ada.shape
    B = c.shape[0]
    nj = D6 // D
    return pl.pallas_call(
        _mod_kernel,
        grid=(L, nj),
        in_specs=[
            pl.BlockSpec((B, D), lambda l, j: (0, 0)),
            pl.BlockSpec((None, D, D), lambda l, j: (l, 0, j)),
            pl.BlockSpec((None, 1, D), lambda l, j: (l, 0, j)),
        ],
        out_specs=pl.BlockSpec((None, B, D), lambda l, j: (l, 0, j)),
        out_shape=jax.ShapeDtypeStruct((L, B, D6), F32),
        name="adaln_mod",
    )(c, w_ada, b_ada.reshape(L, 1, D6))


IN_COLS = (("lru", 512), ("sb", 768), ("gdn", 1024), ("ab", LANES), ("rw", 896))


def _in_kernel(x_ref, mod_ref, g_ref, w_ref, *o_refs, d):
    mod = mod_ref[...]
    h = _modulated_norm(x_ref[...], g_ref[...], mod[:, d:2 * d], mod[:, 0:d]).astype(MXU_DTYPE)
    c0 = 0
    for (_, n), o_ref in zip(IN_COLS, o_refs):
        o_ref[...] = jnp.dot(h, w_ref[:, c0:c0 + n], preferred_element_type=F32)
        c0 += n


def _in_proj(x, mod_l, norm_g, w_in_b, l):
    B, S, D = x.shape
    tm = min(TM_DENSE, S)
    n_in = w_in_b.shape[-1]
    tok = lambda n: pl.BlockSpec((None, tm, n), lambda b, s: (b, s, 0))
    return pl.pallas_call(
        functools.partial(_in_kernel, d=D),
        grid=(B, S // tm),
        in_specs=[
            tok(D),
            pl.BlockSpec((None, 1, 6 * D), lambda b, s: (b, 0, 0)),
            pl.BlockSpec((None, 1, D), lambda b, s: (l, 0, 0)),
            pl.BlockSpec((None, D, n_in), lambda b, s: (l, 0, 0)),
        ],
        out_specs=[tok(n) for _, n in IN_COLS],
        out_shape=[jax.ShapeDtypeStruct((B, S, n), F32) for _, n in IN_COLS],
        compiler_params=pltpu.CompilerParams(
            dimension_semantics=("parallel", "parallel"), vmem_limit_bytes=VMEM_LIMIT),
        name="in_proj",
    )(x, mod_l, norm_g, w_in_b)


def _lru_kernel(p_ref, cw_ref, cb_ref, wr_ref, br_ref, wi_ref, bi_ref, lam_ref, o_ref,
                halo_ref, h_ref, *, ts):
    @pl.when(pl.program_id(1) == 0)
    def _():
        halo_ref[...] = jnp.zeros_like(halo_ref)
        h_ref[...] = jnp.zeros_like(h_ref)

    x_in = p_ref[:, 0:WIDTH]
    y_in = p_ref[:, WIDTH:2 * WIDTH]
    halo = halo_ref[...]
    cw = cw_ref[...]
    u = (cw[3:4] * x_in + cw[2:3] * _shift_rows(x_in, 1, halo)
         + cw[1:2] * _shift_rows(x_in, 2, halo) + cw[0:1] * _shift_rows(x_in, 3, halo)
         + cb_ref[...])
    halo_ref[...] = x_in[ts - SUBLANES:ts]

    ub = u.astype(MXU_DTYPE)
    r = _sigmoid(jnp.dot(ub, wr_ref[...], preferred_element_type=F32) + br_ref[...])
    i = _sigmoid(jnp.dot(ub, wi_ref[...], preferred_element_type=F32) + bi_ref[...])
    log_a = (-LRU_C * r) * _softplus(-lam_ref[...])
    a = jnp.exp(log_a)
    bv = jnp.sqrt(-jnp.tanh(log_a) * (a * a + 1.0)) * (i * u)

    d = 1
    while d < ts:
        bv = a * _shift_fill(bv, d, 0.0) + bv
        a = a * _shift_fill(a, d, 1.0)
        d *= 2
    h = bv + a * h_ref[SUBLANES - 1:SUBLANES]
    h_ref[...] = h[ts - SUBLANES:ts]
    o_ref[...] = h * _gelu_tanh(y_in)


def _lru(p_lru, prm, l):
    B, S, _ = p_lru.shape
    ts = min(TS_LRU, S)
    vec = lambda: pl.BlockSpec((None, 1, WIDTH), lambda b, s: (l, 0, 0))
    mat = lambda r: pl.BlockSpec((None, r, WIDTH), lambda b, s: (l, 0, 0))
    return pl.pallas_call(
        functools.partial(_lru_kernel, ts=ts),
        grid=(B, S // ts),
        in_specs=[
            pl.BlockSpec((None, ts, 2 * WIDTH), lambda b, s: (b, s, 0)),
            mat(4), vec(), mat(WIDTH), vec(), mat(WIDTH), vec(), vec(),
        ],
        out_specs=pl.BlockSpec((None, ts, WIDTH), lambda b, s: (b, s, 0)),
        out_shape=jax.ShapeDtypeStruct((B, S, WIDTH), F32),
        scratch_shapes=[pltpu.VMEM((SUBLANES, WIDTH), F32), pltpu.VMEM((SUBLANES, WIDTH), F32)],
        compiler_params=pltpu.CompilerParams(
            dimension_semantics=("parallel", "arbitrary"), vmem_limit_bytes=VMEM_LIMIT),
        name="rg_lru",
    )(p_lru, prm["lru_conv_w"], prm["lru_conv_b"], prm["lru_wr"], prm["lru_b_r"],
      prm["lru_wi"], prm["lru_b_i"], prm["lru_lambda"])


def _sb_kernel(q_ref, k_ref, v_ref, o_ref, acc_ref, *, tq):
    i = pl.program_id(1)
    q = q_ref[...] * (HEAD_DIM ** -0.5)
    head = _iota2((1, WIDTH), 1) >> 6
    hm = [head == h for h in range(N_HEADS)]
    qh = [jnp.where(hm[h], q, 0.0).astype(MXU_DTYPE) for h in range(N_HEADS)]
    row = _iota2((tq, tq), 0)
    col = _iota2((tq, tq), 1)
    rev_incl = (row >= col).astype(jnp.bfloat16)
    past = col < row
    acc_ref[...] = jnp.zeros_like(acc_ref)

    def tile(start, carries, diag):
        kt = k_ref[pl.ds(start, tq), :].astype(MXU_DTYPE)
        vt = v_ref[pl.ds(start, tq), :].astype(MXU_DTYPE)
        new = []
        pv = None
        for h in range(N_HEADS):
            z = lax.dot_general(qh[h], kt, (((1,), (1,)), ((), ())), preferred_element_type=F32)
            sp = _softplus(z)
            if diag:
                sp = jnp.where(past, sp, 0.0)
            cs = _dot_x01(sp, rev_incl, 2) + carries[h]
            w = jnp.exp(z - cs)
            if diag:
                w = jnp.where(past, w, 0.0)
            o = jnp.dot(w.astype(MXU_DTYPE), vt, preferred_element_type=F32)
            pv = o if pv is None else jnp.where(hm[h], o, pv)
            new.append(cs[:, 0:1])
        acc_ref[...] += pv
        return tuple(new)

    zero = jnp.zeros((tq, 1), F32)
    carries = tile(pl.multiple_of(i * tq, tq), (zero,) * N_HEADS, True)

    def body(it, carries):
        return tile(pl.multiple_of((i - 1 - it) * tq, tq), carries, False)

    lax.fori_loop(0, i, body, carries)
    o_ref[...] = acc_ref[...]


def _sb(p_sb):
    B, S, _ = p_sb.shape
    tq = min(TQ_SB, S)
    return pl.pallas_call(
        functools.partial(_sb_kernel, tq=tq),
        grid=(B, S // tq),
        in_specs=[
            pl.BlockSpec((None, tq, WIDTH), lambda b, i: (b, i, 0)),
            pl.BlockSpec((None, S, WIDTH), lambda b, i: (b, 0, 1)),
            pl.BlockSpec((None, S, WIDTH), lambda b, i: (b, 0, 2)),
        ],
        out_specs=pl.BlockSpec((None, tq, WIDTH), lambda b, i: (b, i, 0)),
        out_shape=jax.ShapeDtypeStruct((B, S, WIDTH), F32),
        scratch_shapes=[pltpu.VMEM((tq, WIDTH), F32)],
        compiler_params=pltpu.CompilerParams(
            dimension_semantics=("parallel", "arbitrary"), vmem_limit_bytes=VMEM_LIMIT),
        name="stick_breaking",
    )(p_sb, p_sb, p_sb)


def _stack_col(x, lane_base):
    xt = jnp.concatenate([x] * N_HEADS, axis=0)
    pick = _iota2(xt.shape, 1) == (_iota2(xt.shape, 0) >> 6) + lane_base
    return jnp.sum(jnp.where(pick, xt, 0.0), axis=1, keepdims=True)


def _gdn_kernel(p_ref, ab_ref, cw_ref, alog_ref, dtb_ref, ng_ref, o_ref,
                halo_ref, st_ref, q_s, k_s, v_s, g_s, b_s, o_s, *, ts):
    @pl.when(pl.program_id(1) == 0)
    def _():
        halo_ref[...] = jnp.zeros_like(halo_ref)
        st_ref[...] = jnp.zeros_like(st_ref)

    qkv = p_ref[:, 0:3 * WIDTH]
    halo = halo_ref[...]
    cw = cw_ref[...]
    c = (cw[3:4] * qkv + cw[2:3] * _shift_rows(qkv, 1, halo)
         + cw[1:2] * _shift_rows(qkv, 2, halo) + cw[0:1] * _shift_rows(qkv, 3, halo))
    halo_ref[...] = qkv[ts - SUBLANES:ts]
    c = _silu(c)
    ones_bd = _head_ones()
    q = c[:, 0:WIDTH]
    k = c[:, WIDTH:2 * WIDTH]
    q_s[...] = q * lax.rsqrt(_head_sum(q * q, ones_bd) + EPS) * (HEAD_DIM ** -0.5)
    k_s[...] = k * lax.rsqrt(_head_sum(k * k, ones_bd) + EPS)
    v_s[...] = c[:, 2 * WIDTH:3 * WIDTH]
    ab = ab_ref[...]
    g_s[...] = -jnp.exp(alog_ref[...]) * _softplus(ab + dtb_ref[...])
    b_s[...] = _sigmoid(ab)

    m = _stack_masks()
    cum_incl = (_iota2((CHUNK, CHUNK), 0) >= _iota2((CHUNK, CHUNK), 1)).astype(jnp.bfloat16)

    def chunk(ci, carry):
        r0 = pl.multiple_of(ci * CHUNK, CHUNK)
        rows = pl.ds(r0, CHUNK)
        gcum = _dot_01x(cum_incl, g_s[rows, :], 3)
        g_col = _stack_col(gcum, 0)
        gl_col = _stack_col(jnp.broadcast_to(gcum[CHUNK - 1:CHUNK], gcum.shape), 0)
        beta_col = _stack_col(b_s[rows, :], N_HEADS)
        qb = _stack_bd(q_s[rows, :], m["same"])
        kb = _stack_bd(k_s[rows, :], m["same"])
        vb = _stack_bd(v_s[rows, :], m["same"])

        gmat = jnp.broadcast_to(g_col, (STACK, STACK))
        decay = jnp.exp(jnp.where(m["incl"], gmat - gmat.T, -1e30))
        k_beta = kb * beta_col
        a = jnp.where(m["strict"], _dot_nt(k_beta, kb) * decay, 0.0)
        t = _inv_unit_lower(a, m)
        eg = jnp.exp(g_col)
        sol = _dot(t, jnp.concatenate([vb * beta_col, k_beta * eg], axis=1))
        u_intra = sol[:, 0:STACK]
        w_state = sol[:, STACK:2 * STACK]
        qk = jnp.where(m["incl"], _dot_nt(qb, kb) * decay, 0.0)
        q_dec = qb * eg
        k_dec = kb * jnp.exp(gl_col - g_col)

        state = st_ref[...]
        u = u_intra - _dot(w_state, state)
        o = _dot(q_dec, state) + _dot(qk, u)
        st_ref[...] = state * jnp.exp(gl_col) + _dot_tn(k_dec, u)
        o_s[rows, :] = _unstack(o)
        return carry

    lax.fori_loop(0, ts // CHUNK, chunk, 0)

    o = o_s[...]
    o = o * lax.rsqrt(_head_sum(o * o, ones_bd) * (1.0 / HEAD_DIM) + EPS) * ng_ref[...]
    o_ref[...] = o * _silu(p_ref[:, 3 * WIDTH:4 * WIDTH])


def _gdn(p_gdn, p_ab, prm, l):
    B, S, _ = p_gdn.shape
    ts = min(TS_REC, S)
    tile = lambda: pltpu.VMEM((ts, WIDTH), F32)
    return pl.pallas_call(
        functools.partial(_gdn_kernel, ts=ts),
        grid=(B, S // ts),
        in_specs=[
            pl.BlockSpec((None, ts, 4 * WIDTH), lambda b, s: (b, s, 0)),
            pl.BlockSpec((None, ts, LANES), lambda b, s: (b, s, 0)),
            pl.BlockSpec((None, 4, 3 * WIDTH), lambda b, s: (l, 0, 0)),
            pl.BlockSpec((None, 1, LANES), lambda b, s: (l, 0, 0)),
            pl.BlockSpec((None, 1, LANES), lambda b, s: (l, 0, 0)),
            pl.BlockSpec((None, 1, WIDTH), lambda b, s: (l, 0, 0)),
        ],
        out_specs=pl.BlockSpec((None, ts, WIDTH), lambda b, s: (b, s, 0)),
        out_shape=jax.ShapeDtypeStruct((B, S, WIDTH), F32),
        scratch_shapes=[pltpu.VMEM((SUBLANES, 3 * WIDTH), F32), pltpu.VMEM((STACK, STACK), F32),
                        tile(), tile(), tile(),
                        pltpu.VMEM((ts, LANES), F32), pltpu.VMEM((ts, LANES), F32), tile()],
        compiler_params=pltpu.CompilerParams(
            dimension_semantics=("parallel", "arbitrary"), vmem_limit_bytes=VMEM_LIMIT),
        name="gated_deltanet",
    )(p_gdn, p_ab, prm["gdn_conv_w"], prm["gdn_a_log"], prm["gdn_dt_bias"], prm["gdn_norm_g"])


def _rw_kernel(p_ref, mu_ref, w0_ref, wup_ref, a0_ref, aup_ref, gup_ref, kk_ref, ka_ref, rk_ref,
               lng_ref, lnb_ref, o_ref,
               halo_ref, st_ref, r_s, k_s, v_s, kk_s, b_s, ld_s, y_s, *, ts):
    @pl.when(pl.program_id(1) == 0)
    def _():
        halo_ref[...] = jnp.zeros_like(halo_ref)
        st_ref[...] = jnp.zeros_like(st_ref)

    p = p_ref[...]
    prev = _shift_rows(p, 1, halo_ref[...])
    halo_ref[...] = p[ts - SUBLANES:ts]
    pf = p + (prev - p) * mu_ref[...]
    r = pf[:, 0:WIDTH]
    k = pf[:, WIDTH:2 * WIDTH]
    v = pf[:, 2 * WIDTH:3 * WIDTH]
    x = pf[:, 3 * WIDTH:3 * WIDTH + LANES]
    lw = _dot(jnp.tanh(x), wup_ref[...])
    la = _dot(x, aup_ref[...])
    gate = _dot(_sigmoid(x), gup_ref[...])
    w_log = -_softplus(-(w0_ref[...] + lw)) - 0.5
    a = _sigmoid(a0_ref[...] + la)
    ones_bd = _head_ones()
    kkv = k * kk_ref[...]
    kk = kkv * lax.rsqrt(_head_sum(kkv * kkv, ones_bd) + EPS)
    k = k * (1.0 + (a - 1.0) * ka_ref[...])
    r_s[...] = r
    k_s[...] = k
    v_s[...] = v
    kk_s[...] = kk
    b_s[...] = kk * a
    ld_s[...] = -jnp.exp(w_log)

    m = _stack_masks()
    cum_incl = (_iota2((CHUNK, CHUNK), 0) >= _iota2((CHUNK, CHUNK), 1)).astype(jnp.bfloat16)

    def chunk(ci, carry):
        r0 = pl.multiple_of(ci * CHUNK, CHUNK)
        rows = pl.ds(r0, CHUNK)
        ld = ld_s[rows, :]
        lg = _dot_01x(cum_incl, ld, 3)
        ll = lg[CHUNK - 1:CHUNK]
        e_neg = jnp.exp(-lg)
        e_rest = jnp.exp(ll - lg)
        kc = k_s[rows, :]
        bc = b_s[rows, :]
        kt = _stack_bd(kk_s[rows, :] * jnp.exp(lg - ld), m["same"])
        kh = _stack_bd(kc * e_neg, m["same"])
        bh = _stack_bd(bc * e_neg, m["same"])
        rt = _stack_bd(r_s[rows, :] * jnp.exp(lg), m["same"])
        kbar = _stack_bd(kc * e_rest, m["same"])
        bbar = _stack_bd(bc * e_rest, m["same"])
        vb = _stack_bd(v_s[rows, :], m["same"])

        a_kb = jnp.where(m["strict"], _dot_nt(kt, bh), 0.0)
        a_kk = jnp.where(m["strict"], _dot_nt(kt, kh), 0.0)
        a_rk = jnp.where(m["incl"], _dot_nt(rt, kh), 0.0)
        a_rb = jnp.where(m["incl"], _dot_nt(rt, bh), 0.0)
        t = _inv_unit_lower(a_kb, m)
        u_in = _dot(t, _dot(a_kk, vb))
        w_st = _dot(t, kt)

        state = st_ref[...]
        u = _dot_nt(w_st, state) + u_in
        y = _dot_nt(rt, state) + _dot(a_rk, vb) - _dot(a_rb, u)
        st_ref[...] = state * jnp.exp(ll) + _dot_tn(vb, kbar) - _dot_tn(u, bbar)
        y_s[rows, :] = _unstack(y)
        return carry

    lax.fori_loop(0, ts // CHUNK, chunk, 0)

    y = y_s[...]
    mean = _head_sum(y, ones_bd) * (1.0 / HEAD_DIM)
    yc = y - mean
    var = _head_sum(yc * yc, ones_bd) * (1.0 / HEAD_DIM)
    yn = yc * lax.rsqrt(var + RW_LN_EPS) * lng_ref[...] + lnb_ref[...]
    bonus = _head_sum(r * k * rk_ref[...], ones_bd) * v
    o_ref[...] = (yn + bonus) * gate


def _rw(p_rw, prm, l):
    B, S, n_rw = p_rw.shape
    ts = min(TS_REC, S)
    tile = lambda: pltpu.VMEM((ts, WIDTH), F32)
    vec = lambda n=WIDTH: pl.BlockSpec((None, 1, n), lambda b, s: (l, 0, 0))
    lora = lambda: pl.BlockSpec((None, LANES, WIDTH), lambda b, s: (l, 0, 0))
    return pl.pallas_call(
        functools.partial(_rw_kernel, ts=ts),
        grid=(B, S // ts),
        in_specs=[
            pl.BlockSpec((None, ts, n_rw), lambda b, s: (b, s, 0)),
            vec(n_rw), vec(), lora(), vec(), lora(), lora(), vec(), vec(), vec(), vec(), vec(),
        ],
        out_specs=pl.BlockSpec((None, ts, WIDTH), lambda b, s: (b, s, 0)),
        out_shape=jax.ShapeDtypeStruct((B, S, WIDTH), F32),
        scratch_shapes=[pltpu.VMEM((SUBLANES, n_rw), F32), pltpu.VMEM((STACK, STACK), F32),
                        tile(), tile(), tile(), tile(), tile(), tile(), tile()],
        compiler_params=pltpu.CompilerParams(
            dimension_semantics=("parallel", "arbitrary"), vmem_limit_bytes=VMEM_LIMIT),
        name="rwkv7",
    )(p_rw, prm["rw_mu"], prm["rw_w0"], prm["rw_wup"], prm["rw_a0"], prm["rw_aup"], prm["rw_gup"],
      prm["rw_k_k"], prm["rw_k_a"], prm["rw_r_k"], prm["rw_ln_g"], prm["rw_ln_b"])


def _out_kernel(x_ref, mod_ref, g_ref, o0_ref, o1_ref, o2_ref, o3_ref,
                wg_ref, bg_ref, wb_ref, wo_ref, xo_ref, *, d):
    x = x_ref[...]
    mod = mod_ref[...]
    h = _modulated_norm(x, g_ref[...], mod[:, d:2 * d], mod[:, 0:d]).astype(MXU_DTYPE)
    mixed = None
    for n, o_ref in enumerate((o0_ref, o1_ref, o2_ref, o3_ref)):
        gate = _sigmoid(jnp.dot(h, wg_ref[n], preferred_element_type=F32) + bg_ref[n])
        y = jnp.dot(o_ref[...].astype(MXU_DTYPE), wb_ref[n], preferred_element_type=F32)
        mixed = gate * y if mixed is None else mixed + gate * y
    out = jnp.dot(mixed.astype(MXU_DTYPE), wo_ref[...], preferred_element_type=F32)
    xo_ref[...] = x + mod[:, 2 * d:3 * d] * out


def _out_proj(x, mod_l, norm_g, branches, prm, l):
    B, S, D = x.shape
    tm = min(TM_DENSE, S)
    tok = lambda n: pl.BlockSpec((None, tm, n), lambda b, s: (b, s, 0))
    const = lambda shape: pl.BlockSpec((None,) + shape, lambda b, s: (l,) + (0,) * len(shape))
    return pl.pallas_call(
        functools.partial(_out_kernel, d=D),
        grid=(B, S // tm),
        in_specs=[
            tok(D),
            pl.BlockSpec((None, 1, 6 * D), lambda b, s: (b, 0, 0)),
            const((1, D)),
            tok(WIDTH), tok(WIDTH), tok(WIDTH), tok(WIDTH),
            const((N_HEADS, D, D)), const((N_HEADS, 1, D)), const((N_HEADS, WIDTH, D)), const((D, D)),
        ],
        out_specs=tok(D),
        out_shape=jax.ShapeDtypeStruct((B, S, D), F32),
        compiler_params=pltpu.CompilerParams(
            dimension_semantics=("parallel", "parallel"), vmem_limit_bytes=VMEM_LIMIT),
        name="out_proj",
    )(x, mod_l, norm_g, *branches, prm["w_gate"], prm["b_gate"], prm["w_branch"], prm["w_out"])


def _ffn_kernel(x_ref, mod_ref, g_ref, wg_ref, wu_ref, cw_ref, wd_ref, fg_ref, xo_ref, halo_ref,
                *, d, tm, fc, final):
    @pl.when(pl.program_id(1) == 0)
    def _():
        halo_ref[...] = jnp.zeros_like(halo_ref)

    x = x_ref[...]
    mod = mod_ref[...]
    h = _modulated_norm(x, g_ref[...], mod[:, 4 * d:5 * d], mod[:, 3 * d:4 * d]).astype(MXU_DTYPE)
    acc = None
    for j in range(FFN_SPLIT):
        cols = slice(j * fc, (j + 1) * fc)
        a_pre = jnp.dot(h, wg_ref[:, cols], preferred_element_type=F32)
        up = jnp.dot(h, wu_ref[:, cols], preferred_element_type=F32)
        halo = halo_ref[:, cols]
        cw = cw_ref[:, cols]
        a = (cw[2:3] * a_pre + cw[1:2] * _shift_rows(a_pre, 1, halo)
             + cw[0:1] * _shift_rows(a_pre, 2, halo))
        halo_ref[:, cols] = a_pre[tm - SUBLANES:tm]
        act = (_gelu_tanh(a) * up).astype(MXU_DTYPE)
        part = jnp.dot(act, wd_ref[cols, :], preferred_element_type=F32)
        acc = part if acc is None else acc + part
    y = x + mod[:, 5 * d:6 * d] * acc
    if final:
        y = y * lax.rsqrt(jnp.mean(y * y, axis=-1, keepdims=True) + EPS) * fg_ref[...]
    xo_ref[...] = y


def _ffn(x, mod_l, norm_g, prm, final_g, l, final):
    B, S, D = x.shape
    F = prm["ffn_w_gate"].shape[-1]
    tm = min(TM_DENSE, S)
    fc = F // FFN_SPLIT
    tok = lambda n: pl.BlockSpec((None, tm, n), lambda b, s: (b, s, 0))
    const = lambda shape: pl.BlockSpec((None,) + shape, lambda b, s: (l,) + (0,) * len(shape))
    return pl.pallas_call(
        functools.partial(_ffn_kernel, d=D, tm=tm, fc=fc, final=final),
        grid=(B, S // tm),
        in_specs=[
            tok(D),
            pl.BlockSpec((None, 1, 6 * D), lambda b, s: (b, 0, 0)),
            const((1, D)),
            const((D, F)), const((D, F)), const((3, F)), const((F, D)),
            pl.BlockSpec((1, D), lambda b, s: (0, 0)),
        ],
        out_specs=tok(D),
        out_shape=jax.ShapeDtypeStruct((B, S, D), F32),
        scratch_shapes=[pltpu.VMEM((SUBLANES, F), F32)],
        compiler_params=pltpu.CompilerParams(
            dimension_semantics=("parallel", "arbitrary"), vmem_limit_bytes=VMEM_LIMIT),
        name="conv_ffn",
    )(x, mod_l, norm_g, prm["ffn_w_gate"], prm["ffn_w_up"], prm["ffn_conv_w"], prm["ffn_w_down"],
      final_g)


def _prepare(w_in, lru_conv_w, lru_conv_b, lru_w_r, lru_b_r, lru_w_i, lru_b_i, lru_lambda,
             gdn_conv_w, gdn_a_log, gdn_dt_bias, gdn_norm_g,
             rw_mu, rw_w0, rw_w_up, rw_a0, rw_a_up, rw_g_up, rw_k_k, rw_k_a, rw_r_k, rw_ln_g, rw_ln_b,
             w_branch, w_gate, b_gate, w_out, ffn_w_gate, ffn_w_up, ffn_conv_w, ffn_w_down):
    L, D, _ = w_in.shape
    bf = lambda t: t.astype(MXU_DTYPE)
    row = lambda t: t.reshape(L, 1, -1)

    n_ab = 2 * N_HEADS
    c_ab = 512 + 768 + 1024
    w_in_p = jnp.concatenate(
        [w_in[:, :, :c_ab], w_in[:, :, c_ab:c_ab + n_ab],
         jnp.zeros((L, D, LANES - n_ab), w_in.dtype), w_in[:, :, c_ab + n_ab:]], axis=-1)

    def block_diag(w):
        eye = jnp.eye(N_HEADS, dtype=w.dtype)
        return jnp.einsum("lnef,nm->lnemf", w, eye).reshape(L, WIDTH, WIDTH)

    def lane_pad(t, lo):
        return jnp.pad(t, ((0, 0), (lo, LANES - lo - t.shape[1]))).reshape(L, 1, LANES)

    def lora_pad(w, lo):
        return jnp.pad(w, ((0, 0), (lo, LANES - lo - w.shape[1]), (0, 0)))

    return dict(
        w_in=bf(w_in_p),
        lru_conv_w=lru_conv_w, lru_conv_b=row(lru_conv_b),
        lru_wr=bf(block_diag(lru_w_r)), lru_b_r=row(lru_b_r),
        lru_wi=bf(block_diag(lru_w_i)), lru_b_i=row(lru_b_i), lru_lambda=row(lru_lambda),
        gdn_conv_w=gdn_conv_w, gdn_a_log=lane_pad(gdn_a_log, 0), gdn_dt_bias=lane_pad(gdn_dt_bias, 0),
        gdn_norm_g=row(jnp.tile(gdn_norm_g, (1, N_HEADS))),
        rw_mu=row(rw_mu), rw_w0=row(rw_w0), rw_a0=row(rw_a0),
        rw_wup=bf(lora_pad(rw_w_up, 0)), rw_aup=bf(lora_pad(rw_a_up, W_LORA)),
        rw_gup=bf(lora_pad(rw_g_up, W_LORA + A_LORA)),
        rw_k_k=row(rw_k_k), rw_k_a=row(rw_k_a), rw_r_k=row(rw_r_k),
        rw_ln_g=row(rw_ln_g), rw_ln_b=row(rw_ln_b),
        w_gate=bf(w_gate), b_gate=b_gate.reshape(L, N_HEADS, 1, D), w_branch=bf(w_branch), w_out=bf(w_out),
        ffn_w_gate=bf(ffn_w_gate), ffn_w_up=bf(ffn_w_up), ffn_conv_w=ffn_conv_w, ffn_w_down=bf(ffn_w_down),
    )


def kernel(x, c, norm1_g, norm2_g, final_g, w_ada, b_ada, w_in, lru_conv_w, lru_conv_b, lru_w_r, lru_b_r, lru_w_i, lru_b_i, lru_lambda, gdn_conv_w, gdn_a_log, gdn_dt_bias, gdn_norm_g, rw_mu, rw_w0, rw_w_up, rw_a0, rw_a_up, rw_g_up, rw_k_k, rw_k_a, rw_r_k, rw_ln_g, rw_ln_b, w_branch, w_gate, b_gate, w_out, ffn_w_gate, ffn_w_up, ffn_conv_w, ffn_w_down):
    B, S, D = x.shape
    L = w_in.shape[0]
    prm = _prepare(w_in, lru_conv_w, lru_conv_b, lru_w_r, lru_b_r, lru_w_i, lru_b_i, lru_lambda,
                   gdn_conv_w, gdn_a_log, gdn_dt_bias, gdn_norm_g,
                   rw_mu, rw_w0, rw_w_up, rw_a0, rw_a_up, rw_g_up, rw_k_k, rw_k_a, rw_r_k,
                   rw_ln_g, rw_ln_b, w_branch, w_gate, b_gate, w_out,
                   ffn_w_gate, ffn_w_up, ffn_conv_w, ffn_w_down)
    mod = _modulation(c, w_ada, b_ada).reshape(L, B, 1, 6 * D)
    n1 = norm1_g.reshape(L, 1, D)
    n2 = norm2_g.reshape(L, 1, D)
    fg = final_g.reshape(1, D)
    for l in range(L):
        p_lru, p_sb, p_gdn, p_ab, p_rw = _in_proj(x, mod[l], n1, prm["w_in"], l)
        branches = (_lru(p_lru, prm, l), _sb(p_sb), _gdn(p_gdn, p_ab, prm, l), _rw(p_rw, prm, l))
        x = _out_proj(x, mod[l], n1, branches, prm, l)
        x = _ffn(x, mod[l], n2, prm, fg, l, final=(l == L - 1))
    return x
```

```python
import functools

import jax
import jax.numpy as jnp
from jax import lax
from jax.experimental import pallas as pl
from jax.experimental.pallas import tpu as pltpu

F32 = jnp.float32
MXU_DTYPE = jnp.bfloat16

HEAD_DIM = 64
N_HEADS = 4
WIDTH = N_HEADS * HEAD_DIM
CHUNK = 64
STACK = N_HEADS * CHUNK
SUBLANES = 8
LANES = 128
EPS = 1e-6
RW_LN_EPS = 64e-5
LRU_C = 8.0
W_LORA, A_LORA, G_LORA = 32, 32, 64
FFN_SPLIT = 2

TM_DENSE = 512
TS_LRU = 512
TS_REC = 256
TQ_SB = 256

VMEM_LIMIT = 56 * 1024 * 1024


def _dot(a, b):
    return jnp.dot(a.astype(MXU_DTYPE), b.astype(MXU_DTYPE), preferred_element_type=F32)


def _dot_nt(a, b):
    return lax.dot_general(a.astype(MXU_DTYPE), b.astype(MXU_DTYPE),
                           (((1,), (1,)), ((), ())), preferred_element_type=F32)


def _dot_tn(a, b):
    return lax.dot_general(a.astype(MXU_DTYPE), b.astype(MXU_DTYPE),
                           (((0,), (0,)), ((), ())), preferred_element_type=F32)


def _split(x, parts):
    out = []
    r = x
    for _ in range(parts - 1):
        p = r.astype(jnp.bfloat16)
        out.append(p)
        r = r - p.astype(F32)
    out.append(r.astype(jnp.bfloat16))
    return out


def _dot_x01(x, m01, parts):
    acc = None
    for p in _split(x, parts):
        t = jnp.dot(p, m01, preferred_element_type=F32)
        acc = t if acc is None else acc + t
    return acc


def _dot_01x(m01, x, parts):
    acc = None
    for p in _split(x, parts):
        t = jnp.dot(m01, p, preferred_element_type=F32)
        acc = t if acc is None else acc + t
    return acc


def _sigmoid(x):
    return jax.nn.sigmoid(x)


def _softplus(x):
    return jnp.maximum(x, 0.0) + jnp.log1p(jnp.exp(-jnp.abs(x)))


def _silu(x):
    return x * _sigmoid(x)


def _gelu_tanh(x):
    return 0.5 * x * (1.0 + jnp.tanh(0.7978845608028654 * (x + 0.044715 * (x * x * x))))


def _iota2(shape, dim):
    return lax.broadcasted_iota(jnp.int32, shape, dim)


def _shift_rows(x, k, halo):
    r = pltpu.roll(x, k, 0)
    hr = pltpu.roll(halo, k, 0)
    first = jnp.where(_iota2(halo.shape, 0) < k, hr, r[0:SUBLANES])
    return jnp.concatenate([first, r[SUBLANES:]], axis=0)


def _shift_fill(x, d, fill):
    n, w = x.shape
    if d % SUBLANES == 0:
        return jnp.concatenate([jnp.full((d, w), fill, x.dtype), x[:n - d]], axis=0)
    r = pltpu.roll(x, d, 0)
    first = jnp.where(_iota2((SUBLANES, w), 0) < d, fill, r[0:SUBLANES])
    return jnp.concatenate([first, r[SUBLANES:]], axis=0)


def _head_ones():
    return ((_iota2((WIDTH, WIDTH), 0) >> 6) == (_iota2((WIDTH, WIDTH), 1) >> 6)).astype(jnp.bfloat16)


def _head_sum(x, ones_bd):
    return _dot_x01(x, ones_bd, 3)


def _stack_masks():
    r = _iota2((STACK, STACK), 0)
    c = _iota2((STACK, STACK), 1)
    same = (r >> 6) == (c >> 6)
    ri = r & (CHUNK - 1)
    ci = c & (CHUNK - 1)
    return dict(
        same=same,
        incl=same & (ri >= ci),
        strict=same & (ri > ci),
        m16=(r >> 4) == (c >> 4),
        m32=(r >> 5) == (c >> 5),
        eye=r == c,
    )


def _stack_bd(x, same):
    return jnp.where(same, jnp.concatenate([x] * N_HEADS, axis=0), 0.0)


def _unstack(y):
    out = y[0:CHUNK]
    for h in range(1, N_HEADS):
        out = out + y[h * CHUNK:(h + 1) * CHUNK]
    return out


def _inv_unit_lower(mats, m):
    n = range(len(mats))
    a0 = [jnp.where(m["m16"], a, 0.0) for a in mats]
    a1 = [jnp.where(m["m32"] & jnp.logical_not(m["m16"]), a, 0.0) for a in mats]
    a2 = [jnp.where(m["m32"], 0.0, a) for a in mats]
    p = [jnp.where(m["eye"], 1.0, 0.0) - a for a in a0]
    x = [_dot(a, a) for a in a0]
    for _ in range(2):
        px = [_dot(jnp.concatenate([p[i], x[i]], axis=0), x[i]) for i in n]
        p = [p[i] + px[i][0:STACK] for i in n]
        x = [px[i][STACK:2 * STACK] for i in n]
    p = [p[i] + _dot(p[i], x[i]) for i in n]
    for off in (a1, a2):
        t = [_dot(p[i], off[i]) for i in n]
        p = [p[i] - _dot(t[i], p[i]) for i in n]
    return p


def _modulated_norm(x, g, scale, shift):
    ms = jnp.mean(x * x, axis=-1, keepdims=True)
    return (x * lax.rsqrt(ms + EPS) * g) * (1.0 + scale) + shift


def _mod_kernel(c_ref, w_ref, b_ref, o_ref):
    c = c_ref[...]
    cond = c * _sigmoid(c)
    o_ref[...] = jnp.dot(cond, w_ref[...], precision=lax.Precision.HIGHEST,
                         preferred_element_type=F32) + b_ref[...]


def _modulation(c, w_ada, b_ada):
    L, D, D6 = w_ada.shape
    B = c.shape[0]
    nj = D6 // D
    return pl.pallas_call(
        _mod_kernel,
        grid=(L, nj),
        in_specs=[
            pl.BlockSpec((B, D), lambda l, j: (0, 0)),
            pl.BlockSpec((None, D, D), lambda l, j: (l, 0, j)),
            pl.BlockSpec((None, 1, D), lambda l, j: (l, 0, j)),
        ],
        out_specs=pl.BlockSpec((None, B, D), lambda l, j: (l, 0, j)),
        out_shape=jax.ShapeDtypeStruct((L, B, D6), F32),
        name="adaln_mod",
    )(c, w_ada, b_ada.reshape(L, 1, D6))


IN_COLS = (("lru", 512), ("sb", 768), ("gdn", 1024), ("ab", LANES), ("rw", 896))


def _in_kernel(x_ref, mod_ref, g_ref, w_ref, *o_refs, d):
    mod = mod_ref[...]
    h = _modulated_norm(x_ref[...], g_ref[...], mod[:, d:2 * d], mod[:, 0:d]).astype(MXU_DTYPE)
    c0 = 0
    for (_, n), o_ref in zip(IN_COLS, o_refs):
        o_ref[...] = jnp.dot(h, w_ref[:, c0:c0 + n], preferred_element_type=F32)
        c0 += n


def _in_proj(x, mod_l, norm_g, w_in_b, l):
    B, S, D = x.shape
    tm = min(TM_DENSE, S)
    n_in = w_in_b.shape[-1]
    tok = lambda n: pl.BlockSpec((None, tm, n), lambda b, s: (b, s, 0))
    return pl.pallas_call(
        functools.partial(_in_kernel, d=D),
        grid=(B, S // tm),
        in_specs=[
            tok(D),
            pl.BlockSpec((None, 1, 6 * D), lambda b, s: (b, 0, 0)),
            pl.BlockSpec((None, 1, D), lambda b, s: (l, 0, 0)),
            pl.BlockSpec((None, D, n_in), lambda b, s: (l, 0, 0)),
        ],
        out_specs=[tok(n) for _, n in IN_COLS],
        out_shape=[jax.ShapeDtypeStruct((B, S, n), F32) for _, n in IN_COLS],
        compiler_params=pltpu.CompilerParams(
            dimension_semantics=("parallel", "parallel"), vmem_limit_bytes=VMEM_LIMIT),
        name="in_proj",
    )(x, mod_l, norm_g, w_in_b)


def _lru_kernel(p_ref, cw_ref, cb_ref, wr_ref, br_ref, wi_ref, bi_ref, lam_ref, o_ref,
                halo_ref, h_ref, *, ts):
    @pl.when(pl.program_id(1) == 0)
    def _():
        halo_ref[...] = jnp.zeros_like(halo_ref)
        h_ref[...] = jnp.zeros_like(h_ref)

    x_in = p_ref[:, 0:WIDTH]
    y_in = p_ref[:, WIDTH:2 * WIDTH]
    halo = halo_ref[...]
    cw = cw_ref[...]
    u = (cw[3:4] * x_in + cw[2:3] * _shift_rows(x_in, 1, halo)
         + cw[1:2] * _shift_rows(x_in, 2, halo) + cw[0:1] * _shift_rows(x_in, 3, halo)
         + cb_ref[...])
    halo_ref[...] = x_in[ts - SUBLANES:ts]

    ub = u.astype(MXU_DTYPE)
    r = _sigmoid(jnp.dot(ub, wr_ref[...], preferred_element_type=F32) + br_ref[...])
    i = _sigmoid(jnp.dot(ub, wi_ref[...], preferred_element_type=F32) + bi_ref[...])
    log_a = (-LRU_C * r) * _softplus(-lam_ref[...])
    a = jnp.exp(log_a)
    bv = jnp.sqrt(-jnp.tanh(log_a) * (a * a + 1.0)) * (i * u)

    d = 1
    while d < ts:
        bv = a * _shift_fill(bv, d, 0.0) + bv
        a = a * _shift_fill(a, d, 1.0)
        d *= 2
    h = bv + a * h_ref[SUBLANES - 1:SUBLANES]
    h_ref[...] = h[ts - SUBLANES:ts]
    o_ref[...] = h * _gelu_tanh(y_in)


def _lru(p_lru, prm, l):
    B, S, _ = p_lru.shape
    ts = min(TS_LRU, S)
    vec = lambda: pl.BlockSpec((None, 1, WIDTH), lambda b, s: (l, 0, 0))
    mat = lambda r: pl.BlockSpec((None, r, WIDTH), lambda b, s: (l, 0, 0))
    return pl.pallas_call(
        functools.partial(_lru_kernel, ts=ts),
        grid=(B, S // ts),
        in_specs=[
            pl.BlockSpec((None, ts, 2 * WIDTH), lambda b, s: (b, s, 0)),
            mat(4), vec(), mat(WIDTH), vec(), mat(WIDTH), vec(), vec(),
        ],
        out_specs=pl.BlockSpec((None, ts, WIDTH), lambda b, s: (b, s, 0)),
        out_shape=jax.ShapeDtypeStruct((B, S, WIDTH), F32),
        scratch_shapes=[pltpu.VMEM((SUBLANES, WIDTH), F32), pltpu.VMEM((SUBLANES, WIDTH), F32)],
        compiler_params=pltpu.CompilerParams(
            dimension_semantics=("parallel", "arbitrary"), vmem_limit_bytes=VMEM_LIMIT),
        name="rg_lru",
    )(p_lru, prm["lru_conv_w"], prm["lru_conv_b"], prm["lru_wr"], prm["lru_b_r"],
      prm["lru_wi"], prm["lru_b_i"], prm["lru_lambda"])


def _sb_kernel(q_ref, k_ref, v_ref, o_ref, acc_ref, *, tq):
    i = pl.program_id(1)
    q = q_ref[...] * (HEAD_DIM ** -0.5)
    head = _iota2((1, WIDTH), 1) >> 6
    hm = [head == h for h in range(N_HEADS)]
    qh = [jnp.where(hm[h], q, 0.0).astype(MXU_DTYPE) for h in range(N_HEADS)]
    row = _iota2((tq, tq), 0)
    col = _iota2((tq, tq), 1)
    rev_incl = (row >= col).astype(jnp.bfloat16)
    past = col < row
    acc_ref[...] = jnp.zeros_like(acc_ref)

    def tile(start, carries, diag):
        kt = k_ref[pl.ds(start, tq), :].astype(MXU_DTYPE)
        vt = v_ref[pl.ds(start, tq), :].astype(MXU_DTYPE)
        heads = range(N_HEADS)
        zs = [lax.dot_general(qh[h], kt, (((1,), (1,)), ((), ())), preferred_element_type=F32)
              for h in heads]
        sps = [jnp.maximum(z, 0.0) + jnp.log(1.0 + jnp.exp(-jnp.abs(z))) for z in zs]
        if diag:
            sps = [jnp.where(past, sp, 0.0) for sp in sps]
        css = [_dot_x01(sps[h], rev_incl, 2) + carries[h] for h in heads]
        ws = [jnp.exp(zs[h] - css[h]) for h in heads]
        if diag:
            ws = [jnp.where(past, w, 0.0) for w in ws]
        pv = None
        for h in heads:
            o = jnp.dot(ws[h].astype(MXU_DTYPE), vt, preferred_element_type=F32)
            pv = o if pv is None else jnp.where(hm[h], o, pv)
        acc_ref[...] += pv
        return tuple(cs[:, 0:1] for cs in css)

    zero = jnp.zeros((tq, 1), F32)
    carries = tile(pl.multiple_of(i * tq, tq), (zero,) * N_HEADS, True)

    def body(it, carries):
        return tile(pl.multiple_of((i - 1 - it) * tq, tq), carries, False)

    lax.fori_loop(0, i, body, carries)
    o_ref[...] = acc_ref[...]


def _sb(p_sb):
    B, S, _ = p_sb.shape
    tq = min(TQ_SB, S)
    return pl.pallas_call(
        functools.partial(_sb_kernel, tq=tq),
        grid=(B, S // tq),
        in_specs=[
            pl.BlockSpec((None, tq, WIDTH), lambda b, i: (b, i, 0)),
            pl.BlockSpec((None, S, WIDTH), lambda b, i: (b, 0, 1)),
            pl.BlockSpec((None, S, WIDTH), lambda b, i: (b, 0, 2)),
        ],
        out_specs=pl.BlockSpec((None, tq, WIDTH), lambda b, i: (b, i, 0)),
        out_shape=jax.ShapeDtypeStruct((B, S, WIDTH), F32),
        scratch_shapes=[pltpu.VMEM((tq, WIDTH), F32)],
        compiler_params=pltpu.CompilerParams(
            dimension_semantics=("parallel", "arbitrary"), vmem_limit_bytes=VMEM_LIMIT),
        name="stick_breaking",
    )(p_sb, p_sb, p_sb)


def _stack_col(x, lane_base):
    xt = jnp.concatenate([x] * N_HEADS, axis=0)
    pick = _iota2(xt.shape, 1) == (_iota2(xt.shape, 0) >> 6) + lane_base
    return jnp.sum(jnp.where(pick, xt, 0.0), axis=1, keepdims=True)


def _gdn_kernel(p_ref, ab_ref, cw_ref, alog_ref, dtb_ref, ng_ref, o_ref,
                halo_ref, st_ref, *, ts):
    @pl.when(pl.program_id(1) == 0)
    def _():
        halo_ref[...] = jnp.zeros_like(halo_ref)
        st_ref[...] = jnp.zeros_like(st_ref)

    qkv = p_ref[:, 0:3 * WIDTH]
    halo = halo_ref[...]
    cw = cw_ref[...]
    c = (cw[3:4] * qkv + cw[2:3] * _shift_rows(qkv, 1, halo)
         + cw[1:2] * _shift_rows(qkv, 2, halo) + cw[0:1] * _shift_rows(qkv, 3, halo))
    halo_ref[...] = qkv[ts - SUBLANES:ts]
    c = _silu(c)
    ones_bd = _head_ones()
    q = c[:, 0:WIDTH]
    k = c[:, WIDTH:2 * WIDTH]
    qn = q * lax.rsqrt(_head_sum(q * q, ones_bd) + EPS) * (HEAD_DIM ** -0.5)
    kn = k * lax.rsqrt(_head_sum(k * k, ones_bd) + EPS)
    v = c[:, 2 * WIDTH:3 * WIDTH]
    ab = ab_ref[...]
    g = -jnp.exp(alog_ref[...]) * _softplus(ab + dtb_ref[...])
    beta = _sigmoid(ab)

    m = _stack_masks()
    cum_incl = (_iota2((CHUNK, CHUNK), 0) >= _iota2((CHUNK, CHUNK), 1)).astype(jnp.bfloat16)
    chunks = range(ts // CHUNK)
    rows = [slice(ci * CHUNK, (ci + 1) * CHUNK) for ci in chunks]

    gcum = [_dot_01x(cum_incl, g[r], 3) for r in rows]
    g_col = [_stack_col(x, 0) for x in gcum]
    gl_col = [_stack_col(jnp.broadcast_to(x[CHUNK - 1:CHUNK], x.shape), 0) for x in gcum]
    beta_col = [_stack_col(beta[r], N_HEADS) for r in rows]
    qb = [_stack_bd(qn[r], m["same"]) for r in rows]
    kb = [_stack_bd(kn[r], m["same"]) for r in rows]
    vb = [_stack_bd(v[r], m["same"]) for r in rows]
    gmat = [jnp.broadcast_to(x, (STACK, STACK)) for x in g_col]
    decay = [jnp.exp(jnp.where(m["incl"], x - x.T, -1e30)) for x in gmat]
    k_beta = [kb[i] * beta_col[i] for i in chunks]
    kq = [_dot_nt(jnp.concatenate([k_beta[i], qb[i]], axis=0), kb[i]) for i in chunks]
    a = [jnp.where(m["strict"], kq[i][0:STACK] * decay[i], 0.0) for i in chunks]
    qk = [jnp.where(m["incl"], kq[i][STACK:2 * STACK] * decay[i], 0.0) for i in chunks]
    t = _inv_unit_lower(a, m)
    eg = [jnp.exp(x) for x in g_col]
    sol = [_dot(t[i], jnp.concatenate([vb[i] * beta_col[i], k_beta[i] * eg[i]], axis=1))
           for i in chunks]
    k_dec = [kb[i] * jnp.exp(gl_col[i] - g_col[i]) for i in chunks]
    nm = [_dot_tn(k_dec[i], sol[i]) for i in chunks]
    wq = [jnp.concatenate([sol[i][:, STACK:2 * STACK], qb[i] * eg[i]], axis=0) for i in chunks]

    state = st_ref[...]
    states = []
    for i in chunks:
        states.append(state)
        state = (state * jnp.exp(gl_col[i]) - _dot(nm[i][:, STACK:2 * STACK], state)
                 + nm[i][:, 0:STACK])
    st_ref[...] = state
    ws = [_dot(wq[i], states[i]) for i in chunks]
    u = [sol[i][:, 0:STACK] - ws[i][0:STACK] for i in chunks]
    o = jnp.concatenate(
        [_unstack(ws[i][STACK:2 * STACK] + _dot(qk[i], u[i])) for i in chunks], axis=0)
    o = o * lax.rsqrt(_head_sum(o * o, ones_bd) * (1.0 / HEAD_DIM) + EPS) * ng_ref[...]
    o_ref[...] = o * _silu(p_ref[:, 3 * WIDTH:4 * WIDTH])


def _gdn(p_gdn, p_ab, prm, l):
    B, S, _ = p_gdn.shape
    ts = min(TS_REC, S)
    return pl.pallas_call(
        functools.partial(_gdn_kernel, ts=ts),
        grid=(B, S // ts),
        in_specs=[
            pl.BlockSpec((None, ts, 4 * WIDTH), lambda b, s: (b, s, 0)),
            pl.BlockSpec((None, ts, LANES), lambda b, s: (b, s, 0)),
            pl.BlockSpec((None, 4, 3 * WIDTH), lambda b, s: (l, 0, 0)),
            pl.BlockSpec((None, 1, LANES), lambda b, s: (l, 0, 0)),
            pl.BlockSpec((None, 1, LANES), lambda b, s: (l, 0, 0)),
            pl.BlockSpec((None, 1, WIDTH), lambda b, s: (l, 0, 0)),
        ],
        out_specs=pl.BlockSpec((None, ts, WIDTH), lambda b, s: (b, s, 0)),
        out_shape=jax.ShapeDtypeStruct((B, S, WIDTH), F32),
        scratch_shapes=[pltpu.VMEM((SUBLANES, 3 * WIDTH), F32), pltpu.VMEM((STACK, STACK), F32)],
        compiler_params=pltpu.CompilerParams(
            dimension_semantics=("parallel", "arbitrary"), vmem_limit_bytes=VMEM_LIMIT),
        name="gated_deltanet",
    )(p_gdn, p_ab, prm["gdn_conv_w"], prm["gdn_a_log"], prm["gdn_dt_bias"], prm["gdn_norm_g"])


def _rw_kernel(p_ref, mu_ref, w0_ref, wup_ref, a0_ref, aup_ref, gup_ref, kk_ref, ka_ref, rk_ref,
               lng_ref, lnb_ref, o_ref,
               halo_ref, st_ref, *, ts):
    @pl.when(pl.program_id(1) == 0)
    def _():
        halo_ref[...] = jnp.zeros_like(halo_ref)
        st_ref[...] = jnp.zeros_like(st_ref)

    p = p_ref[...]
    prev = _shift_rows(p, 1, halo_ref[...])
    halo_ref[...] = p[ts - SUBLANES:ts]
    pf = p + (prev - p) * mu_ref[...]
    r = pf[:, 0:WIDTH]
    k = pf[:, WIDTH:2 * WIDTH]
    v = pf[:, 2 * WIDTH:3 * WIDTH]
    x = pf[:, 3 * WIDTH:3 * WIDTH + LANES]
    lw = _dot(jnp.tanh(x), wup_ref[...])
    la = _dot(x, aup_ref[...])
    gate = _dot(_sigmoid(x), gup_ref[...])
    w_log = -_softplus(-(w0_ref[...] + lw)) - 0.5
    a = _sigmoid(a0_ref[...] + la)
    ones_bd = _head_ones()
    kkv = k * kk_ref[...]
    kk = kkv * lax.rsqrt(_head_sum(kkv * kkv, ones_bd) + EPS)
    k = k * (1.0 + (a - 1.0) * ka_ref[...])
    b = kk * a
    ld = -jnp.exp(w_log)

    m = _stack_masks()
    cum_incl = (_iota2((CHUNK, CHUNK), 0) >= _iota2((CHUNK, CHUNK), 1)).astype(jnp.bfloat16)
    chunks = range(ts // CHUNK)
    rows = [slice(ci * CHUNK, (ci + 1) * CHUNK) for ci in chunks]
    bd = lambda t: _stack_bd(t, m["same"])
    lo, hi = slice(0, STACK), slice(STACK, 2 * STACK)

    lg = [_dot_01x(cum_incl, ld[r_], 3) for r_ in rows]
    ll = [x[CHUNK - 1:CHUNK] for x in lg]
    e_neg = [jnp.exp(-x) for x in lg]
    e_rest = [jnp.exp(ll[i] - lg[i]) for i in chunks]
    kt = [bd(kk[rows[i]] * jnp.exp(lg[i] - ld[rows[i]])) for i in chunks]
    rt = [bd(r[rows[i]] * jnp.exp(lg[i])) for i in chunks]
    kh = [bd(k[rows[i]] * e_neg[i]) for i in chunks]
    bh = [bd(b[rows[i]] * e_neg[i]) for i in chunks]
    kbar = [bd(k[rows[i]] * e_rest[i]) for i in chunks]
    bbar = [bd(b[rows[i]] * e_rest[i]) for i in chunks]
    vb = [bd(v[r_]) for r_ in rows]
    big = [_dot_nt(jnp.concatenate([kt[i], rt[i]], axis=0), jnp.concatenate([bh[i], kh[i]], axis=0))
           for i in chunks]
    a_kb = [jnp.where(m["strict"], x[lo, lo], 0.0) for x in big]
    a_kk = [jnp.where(m["strict"], x[lo, hi], 0.0) for x in big]
    a_rb = [jnp.where(m["incl"], x[hi, lo], 0.0) for x in big]
    a_rk = [jnp.where(m["incl"], x[hi, hi], 0.0) for x in big]
    t = _inv_unit_lower(a_kb, m)
    akv = [_dot(jnp.concatenate([a_kk[i], a_rk[i]], axis=0), vb[i]) for i in chunks]
    tk = [_dot(t[i], jnp.concatenate([akv[i][lo], kt[i]], axis=1)) for i in chunks]
    tb = [_dot_tn(tk[i], bbar[i]) for i in chunks]
    vk = [_dot_tn(vb[i], kbar[i]) for i in chunks]
    wr = [jnp.concatenate([tk[i][:, hi], rt[i]], axis=0) for i in chunks]

    state = st_ref[...]
    states = []
    for i in chunks:
        states.append(state)
        state = state * jnp.exp(ll[i]) - _dot(state, tb[i][hi]) + (vk[i] - tb[i][lo])
    st_ref[...] = state
    ws = [_dot_nt(wr[i], states[i]) for i in chunks]
    u = [ws[i][lo] + tk[i][:, lo] for i in chunks]
    y = jnp.concatenate(
        [_unstack(ws[i][hi] + akv[i][hi] - _dot(a_rb[i], u[i])) for i in chunks], axis=0)
    mean =_head_sum(y, ones_bd) * (1.0 / HEAD_DIM)
    yc = y - mean
    var = _head_sum(yc * yc, ones_bd) * (1.0 / HEAD_DIM)
    yn = yc * lax.rsqrt(var + RW_LN_EPS) * lng_ref[...] + lnb_ref[...]
    bonus = _head_sum(r * k * rk_ref[...], ones_bd) * v
    o_ref[...] = (yn + bonus) * gate


def _rw(p_rw, prm, l):
    B, S, n_rw = p_rw.shape
    ts = min(TS_REC, S)
    vec = lambda n=WIDTH: pl.BlockSpec((None, 1, n), lambda b, s: (l, 0, 0))
    lora = lambda: pl.BlockSpec((None, LANES, WIDTH), lambda b, s: (l, 0, 0))
    return pl.pallas_call(
        functools.partial(_rw_kernel, ts=ts),
        grid=(B, S // ts),
        in_specs=[
            pl.BlockSpec((None, ts, n_rw), lambda b, s: (b, s, 0)),
            vec(n_rw), vec(), lora(), vec(), lora(), lora(), vec(), vec(), vec(), vec(), vec(),
        ],
        out_specs=pl.BlockSpec((None, ts, WIDTH), lambda b, s: (b, s, 0)),
        out_shape=jax.ShapeDtypeStruct((B, S, WIDTH), F32),
        scratch_shapes=[pltpu.VMEM((SUBLANES, n_rw), F32), pltpu.VMEM((STACK, STACK), F32)],
        compiler_params=pltpu.CompilerParams(
            dimension_semantics=("parallel", "arbitrary"), vmem_limit_bytes=VMEM_LIMIT),
        name="rwkv7",
    )(p_rw, prm["rw_mu"], prm["rw_w0"], prm["rw_wup"], prm["rw_a0"], prm["rw_aup"], prm["rw_gup"],
      prm["rw_k_k"], prm["rw_k_a"], prm["rw_r_k"], prm["rw_ln_g"], prm["rw_ln_b"])


def _out_kernel(x_ref, mod_ref, g_ref, o0_ref, o1_ref, o2_ref, o3_ref,
                wg_ref, bg_ref, wb_ref, wo_ref, xo_ref, *, d):
    x = x_ref[...]
    mod = mod_ref[...]
    h = _modulated_norm(x, g_ref[...], mod[:, d:2 * d], mod[:, 0:d]).astype(MXU_DTYPE)
    mixed = None
    for n, o_ref in enumerate((o0_ref, o1_ref, o2_ref, o3_ref)):
        gate = _sigmoid(jnp.dot(h, wg_ref[n], preferred_element_type=F32) + bg_ref[n])
        y = jnp.dot(o_ref[...].astype(MXU_DTYPE), wb_ref[n], preferred_element_type=F32)
        mixed = gate * y if mixed is None else mixed + gate * y
    out = jnp.dot(mixed.astype(MXU_DTYPE), wo_ref[...], preferred_element_type=F32)
    xo_ref[...] = x + mod[:, 2 * d:3 * d] * out


def _out_proj(x, mod_l, norm_g, branches, prm, l):
    B, S, D = x.shape
    tm = min(TM_DENSE, S)
    tok = lambda n: pl.BlockSpec((None, tm, n), lambda b, s: (b, s, 0))
    const = lambda shape: pl.BlockSpec((None,) + shape, lambda b, s: (l,) + (0,) * len(shape))
    return pl.pallas_call(
        functools.partial(_out_kernel, d=D),
        grid=(B, S // tm),
        in_specs=[
            tok(D),
            pl.BlockSpec((None, 1, 6 * D), lambda b, s: (b, 0, 0)),
            const((1, D)),
            tok(WIDTH), tok(WIDTH), tok(WIDTH), tok(WIDTH),
            const((N_HEADS, D, D)), const((N_HEADS, 1, D)), const((N_HEADS, WIDTH, D)), const((D, D)),
        ],
        out_specs=tok(D),
        out_shape=jax.ShapeDtypeStruct((B, S, D), F32),
        compiler_params=pltpu.CompilerParams(
            dimension_semantics=("parallel", "parallel"), vmem_limit_bytes=VMEM_LIMIT),
        name="out_proj",
    )(x, mod_l, norm_g, *branches, prm["w_gate"], prm["b_gate"], prm["w_branch"], prm["w_out"])


def _ffn_kernel(x_ref, mod_ref, g_ref, wg_ref, wu_ref, cw_ref, wd_ref, fg_ref, xo_ref, halo_ref,
                *, d, tm, fc, final):
    @pl.when(pl.program_id(1) == 0)
    def _():
        halo_ref[...] = jnp.zeros_like(halo_ref)

    x = x_ref[...]
    mod = mod_ref[...]
    h = _modulated_norm(x, g_ref[...], mod[:, 4 * d:5 * d], mod[:, 3 * d:4 * d]).astype(MXU_DTYPE)
    acc = None
    for j in range(FFN_SPLIT):
        cols = slice(j * fc, (j + 1) * fc)
        a_pre = jnp.dot(h, wg_ref[:, cols], preferred_element_type=F32)
        up = jnp.dot(h, wu_ref[:, cols], preferred_element_type=F32)
        halo = halo_ref[:, cols]
        cw = cw_ref[:, cols]
        a = (cw[2:3] * a_pre + cw[1:2] * _shift_rows(a_pre, 1, halo)
             + cw[0:1] * _shift_rows(a_pre, 2, halo))
        halo_ref[:, cols] = a_pre[tm - SUBLANES:tm]
        act = (_gelu_tanh(a) * up).astype(MXU_DTYPE)
        part = jnp.dot(act, wd_ref[cols, :], preferred_element_type=F32)
        acc = part if acc is None else acc + part
    y = x + mod[:, 5 * d:6 * d] * acc
    if final:
        y = y * lax.rsqrt(jnp.mean(y * y, axis=-1, keepdims=True) + EPS) * fg_ref[...]
    xo_ref[...] = y


def _ffn(x, mod_l, norm_g, prm, final_g, l, final):
    B, S, D = x.shape
    F = prm["ffn_w_gate"].shape[-1]
    tm = min(TM_DENSE, S)
    fc = F // FFN_SPLIT
    tok = lambda n: pl.BlockSpec((None, tm, n), lambda b, s: (b, s, 0))
    const = lambda shape: pl.BlockSpec((None,) + shape, lambda b, s: (l,) + (0,) * len(shape))
    return pl.pallas_call(
        functools.partial(_ffn_kernel, d=D, tm=tm, fc=fc, final=final),
        grid=(B, S // tm),
        in_specs=[
            tok(D),
            pl.BlockSpec((None, 1, 6 * D), lambda b, s: (b, 0, 0)),
            const((1, D)),
            const((D, F)), const((D, F)), const((3, F)), const((F, D)),
            pl.BlockSpec((1, D), lambda b, s: (0, 0)),
        ],
        out_specs=tok(D),
        out_shape=jax.ShapeDtypeStruct((B, S, D), F32),
        scratch_shapes=[pltpu.VMEM((SUBLANES, F), F32)],
        compiler_params=pltpu.CompilerParams(
            dimension_semantics=("parallel", "arbitrary"), vmem_limit_bytes=VMEM_LIMIT),
        name="conv_ffn",
    )(x, mod_l, norm_g, prm["ffn_w_gate"], prm["ffn_w_up"], prm["ffn_conv_w"], prm["ffn_w_down"],
      final_g)


def _prepare(w_in, lru_conv_w, lru_conv_b, lru_w_r, lru_b_r, lru_w_i, lru_b_i, lru_lambda,
             gdn_conv_w, gdn_a_log, gdn_dt_bias, gdn_norm_g,
             rw_mu, rw_w0, rw_w_up, rw_a0, rw_a_up, rw_g_up, rw_k_k, rw_k_a, rw_r_k, rw_ln_g, rw_ln_b,
             w_branch, w_gate, b_gate, w_out, ffn_w_gate, ffn_w_up, ffn_conv_w, ffn_w_down):
    L, D, _ = w_in.shape
    bf = lambda t: t.astype(MXU_DTYPE)
    row = lambda t: t.reshape(L, 1, -1)

    n_ab = 2 * N_HEADS
    c_ab = 512 + 768 + 1024
    w_in_p = jnp.concatenate(
        [w_in[:, :, :c_ab], w_in[:, :, c_ab:c_ab + n_ab],
         jnp.zeros((L, D, LANES - n_ab), w_in.dtype), w_in[:, :, c_ab + n_ab:]], axis=-1)

    def block_diag(w):
        eye = jnp.eye(N_HEADS, dtype=w.dtype)
        return jnp.einsum("lnef,nm->lnemf", w, eye).reshape(L, WIDTH, WIDTH)

    def lane_pad(t, lo):
        return jnp.pad(t, ((0, 0), (lo, LANES - lo - t.shape[1]))).reshape(L, 1, LANES)

    def lora_pad(w, lo):
        return jnp.pad(w, ((0, 0), (lo, LANES - lo - w.shape[1]), (0, 0)))

    return dict(
        w_in=bf(w_in_p),
        lru_conv_w=lru_conv_w, lru_conv_b=row(lru_conv_b),
        lru_wr=bf(block_diag(lru_w_r)), lru_b_r=row(lru_b_r),
        lru_wi=bf(block_diag(lru_w_i)), lru_b_i=row(lru_b_i), lru_lambda=row(lru_lambda),
        gdn_conv_w=gdn_conv_w, gdn_a_log=lane_pad(gdn_a_log, 0), gdn_dt_bias=lane_pad(gdn_dt_bias, 0),
        gdn_norm_g=row(jnp.tile(gdn_norm_g, (1, N_HEADS))),
        rw_mu=row(rw_mu), rw_w0=row(rw_w0), rw_a0=row(rw_a0),
        rw_wup=bf(lora_pad(rw_w_up, 0)), rw_aup=bf(lora_pad(rw_a_up, W_LORA)),
        rw_gup=bf(lora_pad(rw_g_up, W_LORA + A_LORA)),
        rw_k_k=row(rw_k_k), rw_k_a=row(rw_k_a), rw_r_k=row(rw_r_k),
        rw_ln_g=row(rw_ln_g), rw_ln_b=row(rw_ln_b),
        w_gate=bf(w_gate), b_gate=b_gate.reshape(L, N_HEADS, 1, D), w_branch=bf(w_branch), w_out=bf(w_out),
        ffn_w_gate=bf(ffn_w_gate), ffn_w_up=bf(ffn_w_up), ffn_conv_w=ffn_conv_w, ffn_w_down=bf(ffn_w_down),
    )


def kernel(x, c, norm1_g, norm2_g, final_g, w_ada, b_ada, w_in, lru_conv_w, lru_conv_b, lru_w_r, lru_b_r, lru_w_i, lru_b_i, lru_lambda, gdn_conv_w, gdn_a_log, gdn_dt_bias, gdn_norm_g, rw_mu, rw_w0, rw_w_up, rw_a0, rw_a_up, rw_g_up, rw_k_k, rw_k_a, rw_r_k, rw_ln_g, rw_ln_b, w_branch, w_gate, b_gate, w_out, ffn_w_gate, ffn_w_up, ffn_conv_w, ffn_w_down):
    B, S, D = x.shape
    L = w_in.shape[0]
    prm = _prepare(w_in, lru_conv_w, lru_conv_b, lru_w_r, lru_b_r, lru_w_i, lru_b_i, lru_lambda,
                   gdn_conv_w, gdn_a_log, gdn_dt_bias, gdn_norm_g,
                   rw_mu, rw_w0, rw_w_up, rw_a0, rw_a_up, rw_g_up, rw_k_k, rw_k_a, rw_r_k,
                   rw_ln_g, rw_ln_b, w_branch, w_gate, b_gate, w_out,
                   ffn_w_gate, ffn_w_up, ffn_conv_w, ffn_w_down)
    mod = _modulation(c, w_ada, b_ada).reshape(L, B, 1, 6 * D)
    n1 = norm1_g.reshape(L, 1, D)
    n2 = norm2_g.reshape(L, 1, D)
    fg = final_g.reshape(1, D)
    for l in range(L):
        p_lru, p_sb, p_gdn, p_ab, p_rw = _in_proj(x, mod[l], n1, prm["w_in"], l)
        branches = (_lru(p_lru, prm, l), _sb(p_sb), _gdn(p_gdn, p_ab, prm, l), _rw(p_rw, prm, l))
        x = _out_proj(x, mod[l], n1, branches, prm, l)
        x = _ffn(x, mod[l], n2, prm, fg, l, final=(l == L - 1))
    return x
```

```python
import functools

import jax
import jax.numpy as jnp
from jax import lax
from jax.experimental import pallas as pl
from jax.experimental.pallas import tpu as pltpu

F32 = jnp.float32
MXU_DTYPE = jnp.bfloat16

HEAD_DIM = 64
N_HEADS = 4
WIDTH = N_HEADS * HEAD_DIM
CHUNK = 64
STACK = N_HEADS * CHUNK
SUBLANES = 8
LANES = 128
EPS = 1e-6
RW_LN_EPS = 64e-5
LRU_C = 8.0
W_LORA, A_LORA, G_LORA = 32, 32, 64
FFN_SPLIT = 2

TM_DENSE = 512
TS_LRU = 512
TS_REC = 512
TQ_SB = 256
SB_GROUPS = (4, 2, 1)

VMEM_LIMIT = 56 * 1024 * 1024


def _dot(a, b):
    return jnp.dot(a.astype(MXU_DTYPE), b.astype(MXU_DTYPE), preferred_element_type=F32)


def _dot_nt(a, b):
    return lax.dot_general(a.astype(MXU_DTYPE), b.astype(MXU_DTYPE),
                           (((1,), (1,)), ((), ())), preferred_element_type=F32)


def _dot_tn(a, b):
    return lax.dot_general(a.astype(MXU_DTYPE), b.astype(MXU_DTYPE),
                           (((0,), (0,)), ((), ())), preferred_element_type=F32)


def _split(x, parts):
    out = []
    r = x
    for _ in range(parts - 1):
        p = r.astype(jnp.bfloat16)
        out.append(p)
        r = r - p.astype(F32)
    out.append(r.astype(jnp.bfloat16))
    return out


def _dot_x01(x, m01, parts):
    acc = None
    for p in _split(x, parts):
        t = jnp.dot(p, m01, preferred_element_type=F32)
        acc = t if acc is None else acc + t
    return acc


def _dot_01x(m01, x, parts):
    acc = None
    for p in _split(x, parts):
        t = jnp.dot(m01, p, preferred_element_type=F32)
        acc = t if acc is None else acc + t
    return acc


def _sigmoid(x):
    return jax.nn.sigmoid(x)


def _softplus(x):
    return jnp.maximum(x, 0.0) + jnp.log1p(jnp.exp(-jnp.abs(x)))


def _silu(x):
    return x * _sigmoid(x)


def _gelu_tanh(x):
    return 0.5 * x * (1.0 + jnp.tanh(0.7978845608028654 * (x + 0.044715 * (x * x * x))))


def _iota2(shape, dim):
    return lax.broadcasted_iota(jnp.int32, shape, dim)


def _shift_rows(x, k, halo):
    r = pltpu.roll(x, k, 0)
    hr = pltpu.roll(halo, k, 0)
    first = jnp.where(_iota2(halo.shape, 0) < k, hr, r[0:SUBLANES])
    return jnp.concatenate([first, r[SUBLANES:]], axis=0)


def _shift_fill(x, d, fill):
    n, w = x.shape
    if d % SUBLANES == 0:
        return jnp.concatenate([jnp.full((d, w), fill, x.dtype), x[:n - d]], axis=0)
    r = pltpu.roll(x, d, 0)
    first = jnp.where(_iota2((SUBLANES, w), 0) < d, fill, r[0:SUBLANES])
    return jnp.concatenate([first, r[SUBLANES:]], axis=0)


def _head_ones():
    return ((_iota2((WIDTH, WIDTH), 0) >> 6) == (_iota2((WIDTH, WIDTH), 1) >> 6)).astype(jnp.bfloat16)


def _head_sum(x, ones_bd):
    return _dot_x01(x, ones_bd, 3)


def _sbs_masks():
    t = _iota2((CHUNK, WIDTH), 0)
    j = _iota2((CHUNK, WIDTH), 1) & (CHUNK - 1)
    r = _iota2((STACK, STACK), 0)
    c = _iota2((STACK, STACK), 1)
    return dict(
        incl=t >= j,
        strict=t > j,
        m16=(t >> 4) == (j >> 4),
        m32=(t >> 5) == (j >> 5),
        eye=t == j,
        same=(r >> 6) == (c >> 6),
        same_w=((r >> 6) == (c >> 6)).astype(MXU_DTYPE),
    )


def _bd(x, same):
    return jnp.concatenate([x.astype(MXU_DTYPE)] * N_HEADS, axis=0) * same


def _keep_blocks(x, same):
    return jnp.where(same, x, 0.0)


def _expand_heads(x, lane_base):
    e = (_iota2((LANES, WIDTH), 0) == (_iota2((LANES, WIDTH), 1) >> 6) + lane_base)
    return _dot_x01(x, e.astype(jnp.bfloat16), 3)


def _inv_unit_lower(mats, m):
    n = range(len(mats))
    same = m["same_w"]
    a0 = [jnp.where(m["m16"], a, 0.0) for a in mats]
    a1 = [jnp.where(m["m32"] & jnp.logical_not(m["m16"]), a, 0.0) for a in mats]
    a2 = [jnp.where(m["m32"], 0.0, a) for a in mats]
    p = [jnp.where(m["eye"], 1.0, 0.0) - a for a in a0]
    x = [_dot(a, _bd(a, same)) for a in a0]
    for _ in range(2):
        px = [_dot(jnp.concatenate([p[i], x[i]], axis=0), _bd(x[i], same)) for i in n]
        p = [p[i] + px[i][0:CHUNK] for i in n]
        x = [px[i][CHUNK:2 * CHUNK] for i in n]
    p = [p[i] + _dot(p[i], _bd(x[i], same)) for i in n]
    for off in (a1, a2):
        t = [_dot(p[i], _bd(off[i], same)) for i in n]
        p = [p[i] - _dot(t[i], _bd(p[i], same)) for i in n]
    return p


def _modulated_norm(x, g, scale, shift):
    ms = jnp.mean(x * x, axis=-1, keepdims=True)
    return (x * lax.rsqrt(ms + EPS) * g) * (1.0 + scale) + shift


def _mod_kernel(c_ref, w_ref, b_ref, o_ref):
    c = c_ref[...]
    cond = c * _sigmoid(c)
    o_ref[...] = jnp.dot(cond, w_ref[...], precision=lax.Precision.HIGHEST,
                         preferred_element_type=F32) + b_ref[...]


def _modulation(c, w_ada, b_ada):
    L, D, D6 = w_ada.shape
    B = c.shape[0]
    nj = D6 // D
    return pl.pallas_call(
        _mod_kernel,
        grid=(L, nj),
        in_specs=[
            pl.BlockSpec((B, D), lambda l, j: (0, 0)),
            pl.BlockSpec((None, D, D), lambda l, j: (l, 0, j)),
            pl.BlockSpec((None, 1, D), lambda l, j: (l, 0, j)),
        ],
        out_specs=pl.BlockSpec((None, B, D), lambda l, j: (l, 0, j)),
        out_shape=jax.ShapeDtypeStruct((L, B, D6), F32),
        name="adaln_mod",
    )(c, w_ada, b_ada.reshape(L, 1, D6))


IN_COLS = (("lru", 512), ("sb", 768), ("gdn", 1024), ("ab", LANES), ("rw", 896))


def _in_kernel(x_ref, mod_ref, g_ref, w_ref, *o_refs, d):
    mod = mod_ref[...]
    h = _modulated_norm(x_ref[...], g_ref[...], mod[:, d:2 * d], mod[:, 0:d]).astype(MXU_DTYPE)
    c0 = 0
    for (_, n), o_ref in zip(IN_COLS, o_refs):
        o_ref[...] = jnp.dot(h, w_ref[:, c0:c0 + n], preferred_element_type=F32)
        c0 += n


def _in_proj(x, mod_l, norm_g, w_in_b, l):
    B, S, D = x.shape
    tm = min(TM_DENSE, S)
    n_in = w_in_b.shape[-1]
    tok = lambda n: pl.BlockSpec((None, tm, n), lambda b, s: (b, s, 0))
    return pl.pallas_call(
        functools.partial(_in_kernel, d=D),
        grid=(B, S // tm),
        in_specs=[
            tok(D),
            pl.BlockSpec((None, 1, 6 * D), lambda b, s: (b, 0, 0)),
            pl.BlockSpec((None, 1, D), lambda b, s: (l, 0, 0)),
            pl.BlockSpec((None, D, n_in), lambda b, s: (l, 0, 0)),
        ],
        out_specs=[tok(n) for _, n in IN_COLS],
        out_shape=[jax.ShapeDtypeStruct((B, S, n), F32) for _, n in IN_COLS],
        compiler_params=pltpu.CompilerParams(
            dimension_semantics=("parallel", "parallel"), vmem_limit_bytes=VMEM_LIMIT),
        name="in_proj",
    )(x, mod_l, norm_g, w_in_b)


def _lru_kernel(p_ref, cw_ref, cb_ref, wr_ref, br_ref, wi_ref, bi_ref, lam_ref, o_ref,
                halo_ref, h_ref, *, ts):
    @pl.when(pl.program_id(1) == 0)
    def _():
        halo_ref[...] = jnp.zeros_like(halo_ref)
        h_ref[...] = jnp.zeros_like(h_ref)

    x_in = p_ref[:, 0:WIDTH]
    y_in = p_ref[:, WIDTH:2 * WIDTH]
    halo = halo_ref[...]
    cw = cw_ref[...]
    u = (cw[3:4] * x_in + cw[2:3] * _shift_rows(x_in, 1, halo)
         + cw[1:2] * _shift_rows(x_in, 2, halo) + cw[0:1] * _shift_rows(x_in, 3, halo)
         + cb_ref[...])
    halo_ref[...] = x_in[ts - SUBLANES:ts]

    ub = u.astype(MXU_DTYPE)
    r = _sigmoid(jnp.dot(ub, wr_ref[...], preferred_element_type=F32) + br_ref[...])
    i = _sigmoid(jnp.dot(ub, wi_ref[...], preferred_element_type=F32) + bi_ref[...])
    log_a = (-LRU_C * r) * _softplus(-lam_ref[...])
    a = jnp.exp(log_a)
    bv = jnp.sqrt(-jnp.tanh(log_a) * (a * a + 1.0)) * (i * u)

    d = 1
    while d < ts:
        bv = a * _shift_fill(bv, d, 0.0) + bv
        a = a * _shift_fill(a, d, 1.0)
        d *= 2
    h = bv + a * h_ref[SUBLANES - 1:SUBLANES]
    h_ref[...] = h[ts - SUBLANES:ts]
    o_ref[...] = h * _gelu_tanh(y_in)


def _lru(p_lru, prm, l):
    B, S, _ = p_lru.shape
    ts = min(TS_LRU, S)
    vec = lambda: pl.BlockSpec((None, 1, WIDTH), lambda b, s: (l, 0, 0))
    mat = lambda r: pl.BlockSpec((None, r, WIDTH), lambda b, s: (l, 0, 0))
    return pl.pallas_call(
        functools.partial(_lru_kernel, ts=ts),
        grid=(B, S // ts),
        in_specs=[
            pl.BlockSpec((None, ts, 2 * WIDTH), lambda b, s: (b, s, 0)),
            mat(4), vec(), mat(WIDTH), vec(), mat(WIDTH), vec(), vec(),
        ],
        out_specs=pl.BlockSpec((None, ts, WIDTH), lambda b, s: (b, s, 0)),
        out_shape=jax.ShapeDtypeStruct((B, S, WIDTH), F32),
        scratch_shapes=[pltpu.VMEM((SUBLANES, WIDTH), F32), pltpu.VMEM((SUBLANES, WIDTH), F32)],
        compiler_params=pltpu.CompilerParams(
            dimension_semantics=("parallel", "arbitrary"), vmem_limit_bytes=VMEM_LIMIT),
        name="rg_lru",
    )(p_lru, prm["lru_conv_w"], prm["lru_conv_b"], prm["lru_wr"], prm["lru_b_r"],
      prm["lru_wi"], prm["lru_b_i"], prm["lru_lambda"])


def _sb_kernel(q_ref, k_ref, v_ref, o_ref, acc_ref, *, tq):
    i = pl.program_id(1)
    q = q_ref[...] * (HEAD_DIM ** -0.5)
    head = _iota2((1, WIDTH), 1) >> 6
    hm = [head == h for h in range(N_HEADS)]
    qh = [jnp.where(hm[h], q, 0.0).astype(MXU_DTYPE) for h in range(N_HEADS)]
    row = _iota2((tq, tq), 0)
    col = _iota2((tq, tq), 1)
    rev_incl = (row >= col).astype(jnp.bfloat16)
    past = col < row
    acc_ref[...] = jnp.zeros_like(acc_ref)

    def tiles(first, count, carries, diag):
        heads = range(N_HEADS)
        group = range(count)
        units = [(g, h) for g in group for h in heads]
        starts = [pl.multiple_of((first - g) * tq, tq) for g in group]
        kts = [k_ref[pl.ds(s, tq), :].astype(MXU_DTYPE) for s in starts]
        vts = [v_ref[pl.ds(s, tq), :].astype(MXU_DTYPE) for s in starts]
        zs = {(g, h): lax.dot_general(qh[h], kts[g], (((1,), (1,)), ((), ())),
                                      preferred_element_type=F32) for g, h in units}
        sps = {u: jnp.maximum(zs[u], 0.0) + jnp.log(1.0 + jnp.exp(-jnp.abs(zs[u]))) for u in units}
        if diag:
            sps = {u: jnp.where(past, sps[u], 0.0) for u in units}
        part = {u: _dot_x01(sps[u], rev_incl, 2) for u in units}
        css = {}
        new = []
        for h in heads:
            c = carries[h]
            for g in group:
                css[(g, h)] = part[(g, h)] + c
                c = css[(g, h)][:, 0:1]
            new.append(c)
        ws = {u: jnp.exp(zs[u] - css[u]) for u in units}
        if diag:
            ws = {u: jnp.where(past, ws[u], 0.0) for u in units}
        total = None
        for g in group:
            pv = None
            for h in heads:
                o = jnp.dot(ws[(g, h)].astype(MXU_DTYPE), vts[g], preferred_element_type=F32)
                pv = o if pv is None else jnp.where(hm[h], o, pv)
            total = pv if total is None else total + pv
        acc_ref[...] += total
        return tuple(new)

    zero = jnp.zeros((tq, 1), F32)
    carries = tiles(i, 1, (zero,) * N_HEADS, True)
    first, rest = i - 1, i
    for g in SB_GROUPS:
        n = rest // g
        carries = lax.fori_loop(
            0, n, lambda it, c, first=first, g=g: tiles(first - it * g, g, c, False), carries)
        first, rest = first - n * g, rest - n * g
    o_ref[...] = acc_ref[...]


def _sb(p_sb):
    B, S, _ = p_sb.shape
    tq = min(TQ_SB, S)
    return pl.pallas_call(
        functools.partial(_sb_kernel, tq=tq),
        grid=(B, S // tq),
        in_specs=[
            pl.BlockSpec((None, tq, WIDTH), lambda b, i: (b, i, 0)),
            pl.BlockSpec((None, S, WIDTH), lambda b, i: (b, 0, 1)),
            pl.BlockSpec((None, S, WIDTH), lambda b, i: (b, 0, 2)),
        ],
        out_specs=pl.BlockSpec((None, tq, WIDTH), lambda b, i: (b, i, 0)),
        out_shape=jax.ShapeDtypeStruct((B, S, WIDTH), F32),
        scratch_shapes=[pltpu.VMEM((tq, WIDTH), F32)],
        compiler_params=pltpu.CompilerParams(
            dimension_semantics=("parallel", "arbitrary"), vmem_limit_bytes=VMEM_LIMIT),
        name="stick_breaking",
    )(p_sb, p_sb, p_sb)


def _gdn_kernel(p_ref, ab_ref, cw_ref, alog_ref, dtb_ref, ng_ref, o_ref,
                halo_ref, st_ref, *, ts):
    @pl.when(pl.program_id(1) == 0)
    def _():
        halo_ref[...] = jnp.zeros_like(halo_ref)
        st_ref[...] = jnp.zeros_like(st_ref)

    qkv = p_ref[:, 0:3 * WIDTH]
    halo = halo_ref[...]
    cw = cw_ref[...]
    c = (cw[3:4] * qkv + cw[2:3] * _shift_rows(qkv, 1, halo)
         + cw[1:2] * _shift_rows(qkv, 2, halo) + cw[0:1] * _shift_rows(qkv, 3, halo))
    halo_ref[...] = qkv[ts - SUBLANES:ts]
    c = _silu(c)
    ones_bd = _head_ones()
    q = c[:, 0:WIDTH]
    k = c[:, WIDTH:2 * WIDTH]
    qn = q * lax.rsqrt(_head_sum(q * q, ones_bd) + EPS) * (HEAD_DIM ** -0.5)
    kn = k * lax.rsqrt(_head_sum(k * k, ones_bd) + EPS)
    v = c[:, 2 * WIDTH:3 * WIDTH]
    ab = ab_ref[...]
    g = -jnp.exp(alog_ref[...]) * _softplus(ab + dtb_ref[...])
    beta = _sigmoid(ab)

    gexp = _expand_heads(g, 0)
    bx = _expand_heads(beta, N_HEADS)

    m = _sbs_masks()
    same = m["same_w"]
    cum_incl = (_iota2((CHUNK, CHUNK), 0) >= _iota2((CHUNK, CHUNK), 1)).astype(jnp.bfloat16)
    ones_cc = jnp.ones((CHUNK, CHUNK), jnp.bfloat16)
    chunks = range(ts // CHUNK)
    rows = [slice(ci * CHUNK, (ci + 1) * CHUNK) for ci in chunks]
    lo, hi = slice(0, CHUNK), slice(CHUNK, 2 * CHUNK)
    left, right = slice(0, STACK), slice(STACK, 2 * STACK)

    gx = [_dot_01x(cum_incl, gexp[r], 3) for r in rows]
    g_row = [_dot_01x(ones_cc, jnp.where(m["eye"], x, 0.0), 3) for x in gx]
    decay = [jnp.exp(jnp.where(m["incl"], gx[i] - g_row[i], -1e30)) for i in chunks]
    k_beta = [kn[rows[i]] * bx[rows[i]] for i in chunks]
    kq = [_dot_nt(jnp.concatenate([k_beta[i], qn[rows[i]]], axis=0), _bd(kn[rows[i]], same))
          for i in chunks]
    a = [jnp.where(m["strict"], kq[i][lo] * decay[i], 0.0) for i in chunks]
    qk = [jnp.where(m["incl"], kq[i][hi] * decay[i], 0.0) for i in chunks]
    t = _inv_unit_lower(a, m)
    eg = [jnp.exp(x) for x in gx]
    sol = [_dot(t[i], jnp.concatenate([_bd(v[rows[i]] * bx[rows[i]], same),
                                       _bd(k_beta[i] * eg[i], same)], axis=1))
           for i in chunks]
    gl = [x[CHUNK - 1:CHUNK] for x in gx]
    k_dec = [kn[rows[i]] * jnp.exp(gl[i] - gx[i]) for i in chunks]
    nm = [_dot_tn(k_dec[i], sol[i]) for i in chunks]
    kd_u = [_keep_blocks(x[:, left], m["same"]) for x in nm]
    kd_w = [_keep_blocks(x[:, right], m["same"]) for x in nm]
    wq = [jnp.concatenate([sol[i][:, right], qn[rows[i]] * eg[i]], axis=0) for i in chunks]

    state = st_ref[...]
    states = []
    for i in chunks:
        states.append(state)
        state = state * jnp.exp(gl[i]) - _dot(kd_w[i], state) + kd_u[i]
    st_ref[...] = state
    ws = [_dot(wq[i], states[i]) for i in chunks]
    u = [sol[i][:, left] - ws[i][lo] for i in chunks]
    o = jnp.concatenate([ws[i][hi] + _dot(qk[i], _bd(u[i], same)) for i in chunks], axis=0)
    o = o * lax.rsqrt(_head_sum(o * o, ones_bd) * (1.0 / HEAD_DIM) + EPS) * ng_ref[...]
    o_ref[...] = o * _silu(p_ref[:, 3 * WIDTH:4 * WIDTH])


def _gdn(p_gdn, p_ab, prm, l):
    B, S, _ = p_gdn.shape
    ts = min(TS_REC, S)
    return pl.pallas_call(
        functools.partial(_gdn_kernel, ts=ts),
        grid=(B, S // ts),
        in_specs=[
            pl.BlockSpec((None, ts, 4 * WIDTH), lambda b, s: (b, s, 0)),
            pl.BlockSpec((None, ts, LANES), lambda b, s: (b, s, 0)),
            pl.BlockSpec((None, 4, 3 * WIDTH), lambda b, s: (l, 0, 0)),
            pl.BlockSpec((None, 1, LANES), lambda b, s: (l, 0, 0)),
            pl.BlockSpec((None, 1, LANES), lambda b, s: (l, 0, 0)),
            pl.BlockSpec((None, 1, WIDTH), lambda b, s: (l, 0, 0)),
        ],
        out_specs=pl.BlockSpec((None, ts, WIDTH), lambda b, s: (b, s, 0)),
        out_shape=jax.ShapeDtypeStruct((B, S, WIDTH), F32),
        scratch_shapes=[pltpu.VMEM((SUBLANES, 3 * WIDTH), F32), pltpu.VMEM((STACK, STACK), F32)],
        compiler_params=pltpu.CompilerParams(
            dimension_semantics=("parallel", "arbitrary"), vmem_limit_bytes=VMEM_LIMIT),
        name="gated_deltanet",
    )(p_gdn, p_ab, prm["gdn_conv_w"], prm["gdn_a_log"], prm["gdn_dt_bias"], prm["gdn_norm_g"])


def _rw_kernel(p_ref, mu_ref, w0_ref, wup_ref, a0_ref, aup_ref, gup_ref, kk_ref, ka_ref, rk_ref,
               lng_ref, lnb_ref, o_ref,
               halo_ref, st_ref, *, ts):
    @pl.when(pl.program_id(1) == 0)
    def _():
        halo_ref[...] = jnp.zeros_like(halo_ref)
        st_ref[...] = jnp.zeros_like(st_ref)

    p = p_ref[...]
    prev = _shift_rows(p, 1, halo_ref[...])
    halo_ref[...] = p[ts - SUBLANES:ts]
    pf = p + (prev - p) * mu_ref[...]
    r = pf[:, 0:WIDTH]
    k = pf[:, WIDTH:2 * WIDTH]
    v = pf[:, 2 * WIDTH:3 * WIDTH]
    x = pf[:, 3 * WIDTH:3 * WIDTH + LANES]
    lw = _dot(jnp.tanh(x), wup_ref[...])
    la = _dot(x, aup_ref[...])
    gate = _dot(_sigmoid(x), gup_ref[...])
    w_log = -_softplus(-(w0_ref[...] + lw)) - 0.5
    a = _sigmoid(a0_ref[...] + la)
    ones_bd = _head_ones()
    kkv = k * kk_ref[...]
    kk = kkv * lax.rsqrt(_head_sum(kkv * kkv, ones_bd) + EPS)
    k = k * (1.0 + (a - 1.0) * ka_ref[...])
    b = kk * a
    ld = -jnp.exp(w_log)

    m = _sbs_masks()
    same = m["same_w"]
    cum_incl = (_iota2((CHUNK, CHUNK), 0) >= _iota2((CHUNK, CHUNK), 1)).astype(jnp.bfloat16)
    chunks = range(ts // CHUNK)
    rows = [slice(ci * CHUNK, (ci + 1) * CHUNK) for ci in chunks]
    bd = lambda t: _bd(t, same)
    lo, hi = slice(0, CHUNK), slice(CHUNK, 2 * CHUNK)
    left, right = slice(0, STACK), slice(STACK, 2 * STACK)

    lg = [_dot_01x(cum_incl, ld[r_], 3) for r_ in rows]
    ll = [x[CHUNK - 1:CHUNK] for x in lg]
    e_neg = [jnp.exp(-x) for x in lg]
    e_rest = [jnp.exp(ll[i] - lg[i]) for i in chunks]
    kt = [kk[rows[i]] * jnp.exp(lg[i] - ld[rows[i]]) for i in chunks]
    rt = [r[rows[i]] * jnp.exp(lg[i]) for i in chunks]
    kh = [k[rows[i]] * e_neg[i] for i in chunks]
    bh = [b[rows[i]] * e_neg[i] for i in chunks]
    kbar = [k[rows[i]] * e_rest[i] for i in chunks]
    bbar = [b[rows[i]] * e_rest[i] for i in chunks]
    big = [_dot_nt(jnp.concatenate([kt[i], rt[i]], axis=0),
                   jnp.concatenate([bd(bh[i]), bd(kh[i])], axis=0)) for i in chunks]
    a_kb = [jnp.where(m["strict"], x[lo, left], 0.0) for x in big]
    a_kk = [jnp.where(m["strict"], x[lo, right], 0.0) for x in big]
    a_rb = [jnp.where(m["incl"], x[hi, left], 0.0) for x in big]
    a_rk = [jnp.where(m["incl"], x[hi, right], 0.0) for x in big]
    t = _inv_unit_lower(a_kb, m)
    akv = [_dot(jnp.concatenate([a_kk[i], a_rk[i]], axis=0), bd(v[rows[i]])) for i in chunks]
    tk = [_dot(t[i], jnp.concatenate([bd(akv[i][lo]), bd(kt[i])], axis=1)) for i in chunks]
    tb = [_dot_tn(tk[i], bbar[i]) for i in chunks]
    ub = [_keep_blocks(x[left], m["same"]) for x in tb]
    wb = [_keep_blocks(x[right], m["same"]) for x in tb]
    vk = [_keep_blocks(_dot_tn(v[rows[i]], kbar[i]), m["same"]) for i in chunks]
    wr = [jnp.concatenate([tk[i][:, right], rt[i]], axis=0) for i in chunks]

    state = st_ref[...]
    states = []
    for i in chunks:
        states.append(state)
        state = state * jnp.exp(ll[i]) - _dot(state, wb[i]) + (vk[i] - ub[i])
    st_ref[...] = state
    ws = [_dot_nt(wr[i], states[i]) for i in chunks]
    u = [ws[i][lo] + tk[i][:, left] for i in chunks]
    y = jnp.concatenate(
        [ws[i][hi] + akv[i][hi] - _dot(a_rb[i], bd(u[i])) for i in chunks], axis=0)
    mean =_head_sum(y, ones_bd) * (1.0 / HEAD_DIM)
    yc = y - mean
    var = _head_sum(yc * yc, ones_bd) * (1.0 / HEAD_DIM)
    yn = yc * lax.rsqrt(var + RW_LN_EPS) * lng_ref[...] + lnb_ref[...]
    bonus = _head_sum(r * k * rk_ref[...], ones_bd) * v
    o_ref[...] = (yn + bonus) * gate


def _rw(p_rw, prm, l):
    B, S, n_rw = p_rw.shape
    ts = min(TS_REC, S)
    vec = lambda n=WIDTH: pl.BlockSpec((None, 1, n), lambda b, s: (l, 0, 0))
    lora = lambda: pl.BlockSpec((None, LANES, WIDTH), lambda b, s: (l, 0, 0))
    return pl.pallas_call(
        functools.partial(_rw_kernel, ts=ts),
        grid=(B, S // ts),
        in_specs=[
            pl.BlockSpec((None, ts, n_rw), lambda b, s: (b, s, 0)),
            vec(n_rw), vec(), lora(), vec(), lora(), lora(), vec(), vec(), vec(), vec(), vec(),
        ],
        out_specs=pl.BlockSpec((None, ts, WIDTH), lambda b, s: (b, s, 0)),
        out_shape=jax.ShapeDtypeStruct((B, S, WIDTH), F32),
        scratch_shapes=[pltpu.VMEM((SUBLANES, n_rw), F32), pltpu.VMEM((STACK, STACK), F32)],
        compiler_params=pltpu.CompilerParams(
            dimension_semantics=("parallel", "arbitrary"), vmem_limit_bytes=VMEM_LIMIT),
        name="rwkv7",
    )(p_rw, prm["rw_mu"], prm["rw_w0"], prm["rw_wup"], prm["rw_a0"], prm["rw_aup"], prm["rw_gup"],
      prm["rw_k_k"], prm["rw_k_a"], prm["rw_r_k"], prm["rw_ln_g"], prm["rw_ln_b"])


def _out_kernel(x_ref, mod_ref, g_ref, o0_ref, o1_ref, o2_ref, o3_ref,
                wg_ref, bg_ref, wb_ref, wo_ref, xo_ref, *, d):
    x = x_ref[...]
    mod = mod_ref[...]
    h = _modulated_norm(x, g_ref[...], mod[:, d:2 * d], mod[:, 0:d]).astype(MXU_DTYPE)
    mixed = None
    for n, o_ref in enumerate((o0_ref, o1_ref, o2_ref, o3_ref)):
        gate = _sigmoid(jnp.dot(h, wg_ref[n], preferred_element_type=F32) + bg_ref[n])
        y = jnp.dot(o_ref[...].astype(MXU_DTYPE), wb_ref[n], preferred_element_type=F32)
        mixed = gate * y if mixed is None else mixed + gate * y
    out = jnp.dot(mixed.astype(MXU_DTYPE), wo_ref[...], preferred_element_type=F32)
    xo_ref[...] = x + mod[:, 2 * d:3 * d] * out


def _out_proj(x, mod_l, norm_g, branches, prm, l):
    B, S, D = x.shape
    tm = min(TM_DENSE, S)
    tok = lambda n: pl.BlockSpec((None, tm, n), lambda b, s: (b, s, 0))
    const = lambda shape: pl.BlockSpec((None,) + shape, lambda b, s: (l,) + (0,) * len(shape))
    return pl.pallas_call(
        functools.partial(_out_kernel, d=D),
        grid=(B, S // tm),
        in_specs=[
            tok(D),
            pl.BlockSpec((None, 1, 6 * D), lambda b, s: (b, 0, 0)),
            const((1, D)),
            tok(WIDTH), tok(WIDTH), tok(WIDTH), tok(WIDTH),
            const((N_HEADS, D, D)), const((N_HEADS, 1, D)), const((N_HEADS, WIDTH, D)), const((D, D)),
        ],
        out_specs=tok(D),
        out_shape=jax.ShapeDtypeStruct((B, S, D), F32),
        compiler_params=pltpu.CompilerParams(
            dimension_semantics=("parallel", "parallel"), vmem_limit_bytes=VMEM_LIMIT),
        name="out_proj",
    )(x, mod_l, norm_g, *branches, prm["w_gate"], prm["b_gate"], prm["w_branch"], prm["w_out"])


def _ffn_kernel(x_ref, mod_ref, g_ref, wg_ref, wu_ref, cw_ref, wd_ref, fg_ref, xo_ref, halo_ref,
                *, d, tm, fc, final):
    @pl.when(pl.program_id(1) == 0)
    def _():
        halo_ref[...] = jnp.zeros_like(halo_ref)

    x = x_ref[...]
    mod = mod_ref[...]
    h = _modulated_norm(x, g_ref[...], mod[:, 4 * d:5 * d], mod[:, 3 * d:4 * d]).astype(MXU_DTYPE)
    acc = None
    for j in range(FFN_SPLIT):
        cols = slice(j * fc, (j + 1) * fc)
        a_pre = jnp.dot(h, wg_ref[:, cols], preferred_element_type=F32)
        up = jnp.dot(h, wu_ref[:, cols], preferred_element_type=F32)
        halo = halo_ref[:, cols]
        cw = cw_ref[:, cols]
        a = (cw[2:3] * a_pre + cw[1:2] * _shift_rows(a_pre, 1, halo)
             + cw[0:1] * _shift_rows(a_pre, 2, halo))
        halo_ref[:, cols] = a_pre[tm - SUBLANES:tm]
        act = (_gelu_tanh(a) * up).astype(MXU_DTYPE)
        part = jnp.dot(act, wd_ref[cols, :], preferred_element_type=F32)
        acc = part if acc is None else acc + part
    y = x + mod[:, 5 * d:6 * d] * acc
    if final:
        y = y * lax.rsqrt(jnp.mean(y * y, axis=-1, keepdims=True) + EPS) * fg_ref[...]
    xo_ref[...] = y


def _ffn(x, mod_l, norm_g, prm, final_g, l, final):
    B, S, D = x.shape
    F = prm["ffn_w_gate"].shape[-1]
    tm = min(TM_DENSE, S)
    fc = F // FFN_SPLIT
    tok = lambda n: pl.BlockSpec((None, tm, n), lambda b, s: (b, s, 0))
    const = lambda shape: pl.BlockSpec((None,) + shape, lambda b, s: (l,) + (0,) * len(shape))
    return pl.pallas_call(
        functools.partial(_ffn_kernel, d=D, tm=tm, fc=fc, final=final),
        grid=(B, S // tm),
        in_specs=[
            tok(D),
            pl.BlockSpec((None, 1, 6 * D), lambda b, s: (b, 0, 0)),
            const((1, D)),
            const((D, F)), const((D, F)), const((3, F)), const((F, D)),
            pl.BlockSpec((1, D), lambda b, s: (0, 0)),
        ],
        out_specs=tok(D),
        out_shape=jax.ShapeDtypeStruct((B, S, D), F32),
        scratch_shapes=[pltpu.VMEM((SUBLANES, F), F32)],
        compiler_params=pltpu.CompilerParams(
            dimension_semantics=("parallel", "arbitrary"), vmem_limit_bytes=VMEM_LIMIT),
        name="conv_ffn",
    )(x, mod_l, norm_g, prm["ffn_w_gate"], prm["ffn_w_up"], prm["ffn_conv_w"], prm["ffn_w_down"],
      final_g)


def _prepare(w_in, lru_conv_w, lru_conv_b, lru_w_r, lru_b_r, lru_w_i, lru_b_i, lru_lambda,
             gdn_conv_w, gdn_a_log, gdn_dt_bias, gdn_norm_g,
             rw_mu, rw_w0, rw_w_up, rw_a0, rw_a_up, rw_g_up, rw_k_k, rw_k_a, rw_r_k, rw_ln_g, rw_ln_b,
             w_branch, w_gate, b_gate, w_out, ffn_w_gate, ffn_w_up, ffn_conv_w, ffn_w_down):
    L, D, _ = w_in.shape
    bf = lambda t: t.astype(MXU_DTYPE)
    row = lambda t: t.reshape(L, 1, -1)

    n_ab = 2 * N_HEADS
    c_ab = 512 + 768 + 1024
    w_in_p = jnp.concatenate(
        [w_in[:, :, :c_ab], w_in[:, :, c_ab:c_ab + n_ab],
         jnp.zeros((L, D, LANES - n_ab), w_in.dtype), w_in[:, :, c_ab + n_ab:]], axis=-1)

    def block_diag(w):
        eye = jnp.eye(N_HEADS, dtype=w.dtype)
        return jnp.einsum("lnef,nm->lnemf", w, eye).reshape(L, WIDTH, WIDTH)

    def lane_pad(t, lo):
        return jnp.pad(t, ((0, 0), (lo, LANES - lo - t.shape[1]))).reshape(L, 1, LANES)

    def lora_pad(w, lo):
        return jnp.pad(w, ((0, 0), (lo, LANES - lo - w.shape[1]), (0, 0)))

    return dict(
        w_in=bf(w_in_p),
        lru_conv_w=lru_conv_w, lru_conv_b=row(lru_conv_b),
        lru_wr=bf(block_diag(lru_w_r)), lru_b_r=row(lru_b_r),
        lru_wi=bf(block_diag(lru_w_i)), lru_b_i=row(lru_b_i), lru_lambda=row(lru_lambda),
        gdn_conv_w=gdn_conv_w, gdn_a_log=lane_pad(gdn_a_log, 0), gdn_dt_bias=lane_pad(gdn_dt_bias, 0),
        gdn_norm_g=row(jnp.tile(gdn_norm_g, (1, N_HEADS))),
        rw_mu=row(rw_mu), rw_w0=row(rw_w0), rw_a0=row(rw_a0),
        rw_wup=bf(lora_pad(rw_w_up, 0)), rw_aup=bf(lora_pad(rw_a_up, W_LORA)),
        rw_gup=bf(lora_pad(rw_g_up, W_LORA + A_LORA)),
        rw_k_k=row(rw_k_k), rw_k_a=row(rw_k_a), rw_r_k=row(rw_r_k),
        rw_ln_g=row(rw_ln_g), rw_ln_b=row(rw_ln_b),
        w_gate=bf(w_gate), b_gate=b_gate.reshape(L, N_HEADS, 1, D), w_branch=bf(w_branch), w_out=bf(w_out),
        ffn_w_gate=bf(ffn_w_gate), ffn_w_up=bf(ffn_w_up), ffn_conv_w=ffn_conv_w, ffn_w_down=bf(ffn_w_down),
    )


def kernel(x, c, norm1_g, norm2_g, final_g, w_ada, b_ada, w_in, lru_conv_w, lru_conv_b, lru_w_r, lru_b_r, lru_w_i, lru_b_i, lru_lambda, gdn_conv_w, gdn_a_log, gdn_dt_bias, gdn_norm_g, rw_mu, rw_w0, rw_w_up, rw_a0, rw_a_up, rw_g_up, rw_k_k, rw_k_a, rw_r_k, rw_ln_g, rw_ln_b, w_branch, w_gate, b_gate, w_out, ffn_w_gate, ffn_w_up, ffn_conv_w, ffn_w_down):
    B, S, D = x.shape
    L = w_in.shape[0]
    prm = _prepare(w_in, lru_conv_w, lru_conv_b, lru_w_r, lru_b_r, lru_w_i, lru_b_i, lru_lambda,
                   gdn_conv_w, gdn_a_log, gdn_dt_bias, gdn_norm_g,
                   rw_mu, rw_w0, rw_w_up, rw_a0, rw_a_up, rw_g_up, rw_k_k, rw_k_a, rw_r_k,
                   rw_ln_g, rw_ln_b, w_branch, w_gate, b_gate, w_out,
                   ffn_w_gate, ffn_w_up, ffn_conv_w, ffn_w_down)
    mod = _modulation(c, w_ada, b_ada).reshape(L, B, 1, 6 * D)
    n1 = norm1_g.reshape(L, 1, D)
    n2 = norm2_g.reshape(L, 1, D)
    fg = final_g.reshape(1, D)
    for l in range(L):
        p_lru, p_sb, p_gdn, p_ab, p_rw = _in_proj(x, mod[l], n1, prm["w_in"], l)
        branches = (_lru(p_lru, prm, l), _sb(p_sb), _gdn(p_gdn, p_ab, prm, l), _rw(p_rw, prm, l))
        x = _out_proj(x, mod[l], n1, branches, prm, l)
        x = _ffn(x, mod[l], n2, prm, fg, l, final=(l == L - 1))
    return x
```

```python
import functools

import jax
import jax.numpy as jnp
from jax import lax
from jax.experimental import pallas as pl
from jax.experimental.pallas import tpu as pltpu

F32 = jnp.float32
MXU_DTYPE = jnp.bfloat16

HEAD_DIM = 64
N_HEADS = 4
WIDTH = N_HEADS * HEAD_DIM
CHUNK = 64
STACK = N_HEADS * CHUNK
SUBLANES = 8
LANES = 128
EPS = 1e-6
RW_LN_EPS = 64e-5
LRU_C = 8.0
W_LORA, A_LORA, G_LORA = 32, 32, 64
FFN_SPLIT = 2

TM_DENSE = 512
TS_REC = 1024
TQ_SB = 256
SB_GROUPS = (4, 2, 1)

VMEM_LIMIT = 56 * 1024 * 1024


def _dot(a, b):
    return lax.dot_general(a, b.astype(MXU_DTYPE), (((1,), (0,)), ((), ())),
                           preferred_element_type=F32)


def _dot_nt(a, b):
    return lax.dot_general(a.astype(MXU_DTYPE), b.astype(MXU_DTYPE),
                           (((1,), (1,)), ((), ())), preferred_element_type=F32)


def _dot_tn(a, b):
    return lax.dot_general(a.astype(MXU_DTYPE), b.astype(MXU_DTYPE),
                           (((0,), (0,)), ((), ())), preferred_element_type=F32)


def _split(x, parts):
    out = []
    r = x
    for _ in range(parts - 1):
        p = r.astype(jnp.bfloat16)
        out.append(p)
        r = r - p.astype(F32)
    out.append(r.astype(jnp.bfloat16))
    return out


def _dot_x01(x, m01, parts):
    acc = None
    for p in _split(x, parts):
        t = jnp.dot(p, m01, preferred_element_type=F32)
        acc = t if acc is None else acc + t
    return acc


def _dot_01x(m01, x, parts):
    acc = None
    for p in _split(x, parts):
        t = jnp.dot(m01, p, preferred_element_type=F32)
        acc = t if acc is None else acc + t
    return acc


def _sigmoid(x):
    return jax.nn.sigmoid(x)


def _softplus(x):
    return jnp.maximum(x, 0.0) + jnp.log1p(jnp.exp(-jnp.abs(x)))


def _silu(x):
    return x * _sigmoid(x)


def _gelu_tanh(x):
    return 0.5 * x * (1.0 + jnp.tanh(0.7978845608028654 * (x + 0.044715 * (x * x * x))))


def _iota2(shape, dim):
    return lax.broadcasted_iota(jnp.int32, shape, dim)


def _shift_rows(x, k, halo):
    r = pltpu.roll(x, k, 0)
    hr = pltpu.roll(halo, k, 0)
    first = jnp.where(_iota2(halo.shape, 0) < k, hr, r[0:SUBLANES])
    return jnp.concatenate([first, r[SUBLANES:]], axis=0)


def _shift_fill(x, d, fill):
    n, w = x.shape
    if d % SUBLANES == 0:
        return jnp.concatenate([jnp.full((d, w), fill, x.dtype), x[:n - d]], axis=0)
    r = pltpu.roll(x, d, 0)
    first = jnp.where(_iota2((SUBLANES, w), 0) < d, fill, r[0:SUBLANES])
    return jnp.concatenate([first, r[SUBLANES:]], axis=0)


def _head_ones():
    return ((_iota2((WIDTH, WIDTH), 0) >> 6) == (_iota2((WIDTH, WIDTH), 1) >> 6)).astype(jnp.bfloat16)


def _head_sum(x, ones_bd):
    return _dot_x01(x, ones_bd, 3)


def _sbs_masks():
    t = _iota2((CHUNK, WIDTH), 0)
    j = _iota2((CHUNK, WIDTH), 1) & (CHUNK - 1)
    r = _iota2((STACK, STACK), 0)
    c = _iota2((STACK, STACK), 1)
    return dict(
        incl=t >= j,
        strict=t > j,
        m16=(t >> 4) == (j >> 4),
        m32=(t >> 5) == (j >> 5),
        eye=t == j,
        same_w=((r >> 6) == (c >> 6)).astype(MXU_DTYPE),
    )


def _bd(x, same):
    return jnp.concatenate([x.astype(MXU_DTYPE)] * N_HEADS, axis=0) * same


def _diag_blocks(full):
    lane_head = _iota2((CHUNK, WIDTH), 1) >> 6
    out = full[0:CHUNK]
    for h in range(1, N_HEADS):
        out = jnp.where(lane_head == h, full[h * CHUNK:(h + 1) * CHUNK], out)
    return out


def _expand_heads(x, lane_base):
    e = (_iota2((LANES, WIDTH), 0) == (_iota2((LANES, WIDTH), 1) >> 6) + lane_base)
    return _dot_x01(x, e.astype(jnp.bfloat16), 3)


def _inv_unit_lower(mats, m):
    n = range(len(mats))
    same = m["same_w"]
    a0 = [jnp.where(m["m16"], a, 0.0) for a in mats]
    a1 = [jnp.where(m["m32"] & jnp.logical_not(m["m16"]), a, 0.0) for a in mats]
    a2 = [jnp.where(m["m32"], 0.0, a) for a in mats]
    p = [jnp.where(m["eye"], 1.0, 0.0) - a for a in a0]
    x = [_dot(a, _bd(a, same)) for a in a0]
    for _ in range(2):
        px = [_dot(jnp.concatenate([p[i], x[i]], axis=0), _bd(x[i], same)) for i in n]
        p = [p[i] + px[i][0:CHUNK] for i in n]
        x = [px[i][CHUNK:2 * CHUNK] for i in n]
    p = [p[i] + _dot(p[i], _bd(x[i], same)) for i in n]
    for off in (a1, a2):
        t = [_dot(p[i], _bd(off[i], same)) for i in n]
        p = [p[i] - _dot(t[i], _bd(p[i], same)) for i in n]
    return p


def _modulated_norm(x, g, scale, shift):
    ms = jnp.mean(x * x, axis=-1, keepdims=True)
    return (x * lax.rsqrt(ms + EPS) * g) * (1.0 + scale) + shift


def _mod_kernel(c_ref, w_ref, b_ref, o_ref):
    c = c_ref[...]
    cond = c * _sigmoid(c)
    o_ref[...] = jnp.dot(cond, w_ref[...], precision=lax.Precision.HIGHEST,
                         preferred_element_type=F32) + b_ref[...]


def _modulation(c, w_ada, b_ada):
    L, D, D6 = w_ada.shape
    B = c.shape[0]
    nj = D6 // D
    return pl.pallas_call(
        _mod_kernel,
        grid=(L, nj),
        in_specs=[
            pl.BlockSpec((B, D), lambda l, j: (0, 0)),
            pl.BlockSpec((None, D, D), lambda l, j: (l, 0, j)),
            pl.BlockSpec((None, 1, D), lambda l, j: (l, 0, j)),
        ],
        out_specs=pl.BlockSpec((None, B, D), lambda l, j: (l, 0, j)),
        out_shape=jax.ShapeDtypeStruct((L, B, D6), F32),
        name="adaln_mod",
    )(c, w_ada, b_ada.reshape(L, 1, D6))


LRU_COLS = 2 * WIDTH
IN_COLS = (("sb", 768), ("gdn", 1024), ("ab", LANES), ("rw", 896))


def _rg_lru(p, cw, cb, wr, br, wi, bi, lam, halo_ref, h_ref, ts):
    x_in = p[:, 0:WIDTH]
    y_in = p[:, WIDTH:2 * WIDTH]
    halo = halo_ref[...]
    u = (cw[3:4] * x_in + cw[2:3] * _shift_rows(x_in, 1, halo)
         + cw[1:2] * _shift_rows(x_in, 2, halo) + cw[0:1] * _shift_rows(x_in, 3, halo) + cb)
    halo_ref[...] = x_in[ts - SUBLANES:ts]

    ub = u.astype(MXU_DTYPE)
    r = _sigmoid(jnp.dot(ub, wr, preferred_element_type=F32) + br)
    i = _sigmoid(jnp.dot(ub, wi, preferred_element_type=F32) + bi)
    log_a = (-LRU_C * r) * _softplus(-lam)
    a = jnp.exp(log_a)
    bv = jnp.sqrt(-jnp.tanh(log_a) * (a * a + 1.0)) * (i * u)

    d = 1
    while d < ts:
        bv = a * _shift_fill(bv, d, 0.0) + bv
        a = a * _shift_fill(a, d, 1.0)
        d *= 2
    h = bv + a * h_ref[SUBLANES - 1:SUBLANES]
    h_ref[...] = h[ts - SUBLANES:ts]
    return h * _gelu_tanh(y_in)


def _in_kernel(x_ref, mod_ref, g_ref, w_ref, cw_ref, cb_ref, wr_ref, br_ref, wi_ref, bi_ref,
               lam_ref, o_lru, *rest, d, tm):
    o_refs, (halo_ref, h_ref) = rest[:len(IN_COLS)], rest[len(IN_COLS):]

    @pl.when(pl.program_id(1) == 0)
    def _():
        halo_ref[...] = jnp.zeros_like(halo_ref)
        h_ref[...] = jnp.zeros_like(h_ref)

    mod = mod_ref[...]
    h = _modulated_norm(x_ref[...], g_ref[...], mod[:, d:2 * d], mod[:, 0:d]).astype(MXU_DTYPE)
    p_lru = jnp.dot(h, w_ref[:, 0:LRU_COLS], preferred_element_type=F32)
    o_lru[...] = _rg_lru(p_lru, cw_ref[...], cb_ref[...], wr_ref[...], br_ref[...], wi_ref[...],
                         bi_ref[...], lam_ref[...], halo_ref, h_ref, tm)
    c0 = LRU_COLS
    for (_, n), o_ref in zip(IN_COLS, o_refs):
        o_ref[...] = jnp.dot(h, w_ref[:, c0:c0 + n], preferred_element_type=F32)
        c0 += n


def _in_proj(x, mod_l, norm_g, prm, l):
    B, S, D = x.shape
    tm = min(TM_DENSE, S)
    n_in = prm["w_in"].shape[-1]
    tok = lambda n: pl.BlockSpec((None, tm, n), lambda b, s: (b, s, 0))
    vec = lambda: pl.BlockSpec((None, 1, WIDTH), lambda b, s: (l, 0, 0))
    mat = lambda r: pl.BlockSpec((None, r, WIDTH), lambda b, s: (l, 0, 0))
    return pl.pallas_call(
        functools.partial(_in_kernel, d=D, tm=tm),
        grid=(B, S // tm),
        in_specs=[
            tok(D),
            pl.BlockSpec((None, 1, 6 * D), lambda b, s: (b, 0, 0)),
            pl.BlockSpec((None, 1, D), lambda b, s: (l, 0, 0)),
            pl.BlockSpec((None, D, n_in), lambda b, s: (l, 0, 0)),
            mat(4), vec(), mat(WIDTH), vec(), mat(WIDTH), vec(), vec(),
        ],
        out_specs=[tok(WIDTH)] + [tok(n) for _, n in IN_COLS],
        out_shape=[jax.ShapeDtypeStruct((B, S, WIDTH), F32)]
        + [jax.ShapeDtypeStruct((B, S, n), F32) for _, n in IN_COLS],
        scratch_shapes=[pltpu.VMEM((SUBLANES, WIDTH), F32), pltpu.VMEM((SUBLANES, WIDTH), F32)],
        compiler_params=pltpu.CompilerParams(
            dimension_semantics=("parallel", "arbitrary"), vmem_limit_bytes=VMEM_LIMIT),
        name="in_proj_lru",
    )(x, mod_l, norm_g, prm["w_in"], prm["lru_conv_w"], prm["lru_conv_b"], prm["lru_wr"],
      prm["lru_b_r"], prm["lru_wi"], prm["lru_b_i"], prm["lru_lambda"])


def _sb_kernel(q_ref, k_ref, v_ref, o_ref, acc_ref, *, tq):
    i = pl.program_id(1)
    q = q_ref[...] * (HEAD_DIM ** -0.5)
    head = _iota2((1, WIDTH), 1) >> 6
    hm = [head == h for h in range(N_HEADS)]
    qh = [jnp.where(hm[h], q, 0.0).astype(MXU_DTYPE) for h in range(N_HEADS)]
    row = _iota2((tq, tq), 0)
    col = _iota2((tq, tq), 1)
    rev_incl = (row >= col).astype(jnp.bfloat16)
    past = col < row
    acc_ref[...] = jnp.zeros_like(acc_ref)

    def tiles(first, count, carries, diag):
        heads = range(N_HEADS)
        group = range(count)
        units = [(g, h) for g in group for h in heads]
        starts = [pl.multiple_of((first - g) * tq, tq) for g in group]
        kts = [k_ref[pl.ds(s, tq), :].astype(MXU_DTYPE) for s in starts]
        vts = [v_ref[pl.ds(s, tq), :].astype(MXU_DTYPE) for s in starts]
        zs = {(g, h): lax.dot_general(qh[h], kts[g], (((1,), (1,)), ((), ())),
                                      preferred_element_type=F32) for g, h in units}
        sps = {u: jnp.maximum(zs[u], 0.0) + jnp.log(1.0 + jnp.exp(-jnp.abs(zs[u]))) for u in units}
        if diag:
            sps = {u: jnp.where(past, sps[u], 0.0) for u in units}
        part = {u: _dot_x01(sps[u], rev_incl, 2) for u in units}
        css = {}
        new = []
        for h in heads:
            c = carries[h]
            for g in group:
                css[(g, h)] = part[(g, h)] + c
                c = css[(g, h)][:, 0:1]
            new.append(c)
        ws = {u: jnp.exp(zs[u] - css[u]) for u in units}
        if diag:
            ws = {u: jnp.where(past, ws[u], 0.0) for u in units}
        total = None
        for g in group:
            pv = None
            for h in heads:
                o = jnp.dot(ws[(g, h)].astype(MXU_DTYPE), vts[g], preferred_element_type=F32)
                pv = o if pv is None else jnp.where(hm[h], o, pv)
            total = pv if total is None else total + pv
        acc_ref[...] += total
        return tuple(new)

    zero = jnp.zeros((tq, 1), F32)
    carries = tiles(i, 1, (zero,) * N_HEADS, True)
    first, rest = i - 1, i
    for g in SB_GROUPS:
        n = rest // g
        carries = lax.fori_loop(
            0, n, lambda it, c, first=first, g=g: tiles(first - it * g, g, c, False), carries)
        first, rest = first - n * g, rest - n * g
    o_ref[...] = acc_ref[...]


def _sb(p_sb):
    B, S, _ = p_sb.shape
    tq = min(TQ_SB, S)
    return pl.pallas_call(
        functools.partial(_sb_kernel, tq=tq),
        grid=(B, S // tq),
        in_specs=[
            pl.BlockSpec((None, tq, WIDTH), lambda b, i: (b, i, 0)),
            pl.BlockSpec((None, S, WIDTH), lambda b, i: (b, 0, 1)),
            pl.BlockSpec((None, S, WIDTH), lambda b, i: (b, 0, 2)),
        ],
        out_specs=pl.BlockSpec((None, tq, WIDTH), lambda b, i: (b, i, 0)),
        out_shape=jax.ShapeDtypeStruct((B, S, WIDTH), F32),
        scratch_shapes=[pltpu.VMEM((tq, WIDTH), F32)],
        compiler_params=pltpu.CompilerParams(
            dimension_semantics=("parallel", "arbitrary"), vmem_limit_bytes=VMEM_LIMIT),
        name="stick_breaking",
    )(p_sb, p_sb, p_sb)


def _gdn_kernel(p_ref, ab_ref, cw_ref, alog_ref, dtb_ref, ng_ref, o_ref,
                halo_ref, st_ref, *, ts):
    @pl.when(pl.program_id(1) == 0)
    def _():
        halo_ref[...] = jnp.zeros_like(halo_ref)
        st_ref[...] = jnp.zeros_like(st_ref)

    qkv = p_ref[:, 0:3 * WIDTH]
    halo = halo_ref[...]
    cw = cw_ref[...]
    c = (cw[3:4] * qkv + cw[2:3] * _shift_rows(qkv, 1, halo)
         + cw[1:2] * _shift_rows(qkv, 2, halo) + cw[0:1] * _shift_rows(qkv, 3, halo))
    halo_ref[...] = qkv[ts - SUBLANES:ts]
    c = _silu(c)
    ones_bd = _head_ones()
    q = c[:, 0:WIDTH]
    k = c[:, WIDTH:2 * WIDTH]
    qn = q * lax.rsqrt(_head_sum(q * q, ones_bd) + EPS) * (HEAD_DIM ** -0.5)
    kn = k * lax.rsqrt(_head_sum(k * k, ones_bd) + EPS)
    v = c[:, 2 * WIDTH:3 * WIDTH]
    ab = ab_ref[...]
    g = -jnp.exp(alog_ref[...]) * _softplus(ab + dtb_ref[...])
    beta = _sigmoid(ab)

    gexp = _expand_heads(g, 0)
    bx = _expand_heads(beta, N_HEADS)

    m = _sbs_masks()
    same = m["same_w"]
    cum_incl = (_iota2((CHUNK, CHUNK), 0) >= _iota2((CHUNK, CHUNK), 1)).astype(jnp.bfloat16)
    ones_cc = jnp.ones((CHUNK, CHUNK), jnp.bfloat16)
    chunks = range(ts // CHUNK)
    rows = [slice(ci * CHUNK, (ci + 1) * CHUNK) for ci in chunks]
    lo, hi = slice(0, CHUNK), slice(CHUNK, 2 * CHUNK)
    left, right = slice(0, STACK), slice(STACK, 2 * STACK)

    gx = [_dot_01x(cum_incl, gexp[r], 3) for r in rows]
    g_row = [_dot_01x(ones_cc, jnp.where(m["eye"], x, 0.0), 3) for x in gx]
    decay = [jnp.exp(jnp.where(m["incl"], gx[i] - g_row[i], -1e30)) for i in chunks]
    k_beta = [kn[rows[i]] * bx[rows[i]] for i in chunks]
    kq = [_dot_nt(jnp.concatenate([k_beta[i], qn[rows[i]]], axis=0), _bd(kn[rows[i]], same))
          for i in chunks]
    a = [jnp.where(m["strict"], kq[i][lo] * decay[i], 0.0) for i in chunks]
    qk = [jnp.where(m["incl"], kq[i][hi] * decay[i], 0.0) for i in chunks]
    t = _inv_unit_lower(a, m)
    eg = [jnp.exp(x) for x in gx]
    sol = [_dot(t[i], jnp.concatenate([_bd(v[rows[i]] * bx[rows[i]], same),
                                       _bd(k_beta[i] * eg[i], same)], axis=1))
           for i in chunks]
    gl = [x[CHUNK - 1:CHUNK] for x in gx]
    k_dec = [kn[rows[i]] * jnp.exp(gl[i] - gx[i]) for i in chunks]
    nm = [_dot_tn(sol[i], k_dec[i]) for i in chunks]
    ut_kd = [_diag_blocks(x[left]) for x in nm]
    wt_kd = [_bd(_diag_blocks(x[right]), same) for x in nm]
    wq = [jnp.concatenate([sol[i][:, right], qn[rows[i]] * eg[i]], axis=0) for i in chunks]

    state = st_ref[...]
    ws, outs = [], []

    def finish(i):
        u = sol[i][:, left] - ws[i][lo]
        outs.append(ws[i][hi] + _dot(qk[i], _bd(u, same)))

    for i in chunks:
        prev = state
        state = state * jnp.exp(gl[i]) - _dot(state, wt_kd[i]) + ut_kd[i]
        ws.append(_dot_nt(wq[i], _bd(prev, same)))
        if i > 0:
            finish(i - 1)
    finish(chunks[-1])
    st_ref[...] = state
    o = jnp.concatenate(outs, axis=0)
    o = o * lax.rsqrt(_head_sum(o * o, ones_bd) * (1.0 / HEAD_DIM) + EPS) * ng_ref[...]
    o_ref[...] = o * _silu(p_ref[:, 3 * WIDTH:4 * WIDTH])


def _gdn(p_gdn, p_ab, prm, l):
    B, S, _ = p_gdn.shape
    ts = min(TS_REC, S)
    return pl.pallas_call(
        functools.partial(_gdn_kernel, ts=ts),
        grid=(B, S // ts),
        in_specs=[
            pl.BlockSpec((None, ts, 4 * WIDTH), lambda b, s: (b, s, 0)),
            pl.BlockSpec((None, ts, LANES), lambda b, s: (b, s, 0)),
            pl.BlockSpec((None, 4, 3 * WIDTH), lambda b, s: (l, 0, 0)),
            pl.BlockSpec((None, 1, LANES), lambda b, s: (l, 0, 0)),
            pl.BlockSpec((None, 1, LANES), lambda b, s: (l, 0, 0)),
            pl.BlockSpec((None, 1, WIDTH), lambda b, s: (l, 0, 0)),
        ],
        out_specs=pl.BlockSpec((None, ts, WIDTH), lambda b, s: (b, s, 0)),
        out_shape=jax.ShapeDtypeStruct((B, S, WIDTH), F32),
        scratch_shapes=[pltpu.VMEM((SUBLANES, 3 * WIDTH), F32), pltpu.VMEM((CHUNK, WIDTH), F32)],
        compiler_params=pltpu.CompilerParams(
            dimension_semantics=("parallel", "arbitrary"), vmem_limit_bytes=VMEM_LIMIT),
        name="gated_deltanet",
    )(p_gdn, p_ab, prm["gdn_conv_w"], prm["gdn_a_log"], prm["gdn_dt_bias"], prm["gdn_norm_g"])


def _rw_kernel(p_ref, mu_ref, w0_ref, wup_ref, a0_ref, aup_ref, gup_ref, kk_ref, ka_ref, rk_ref,
               lng_ref, lnb_ref, o_ref,
               halo_ref, st_ref, *, ts):
    @pl.when(pl.program_id(1) == 0)
    def _():
        halo_ref[...] = jnp.zeros_like(halo_ref)
        st_ref[...] = jnp.zeros_like(st_ref)

    p = p_ref[...]
    prev = _shift_rows(p, 1, halo_ref[...])
    halo_ref[...] = p[ts - SUBLANES:ts]
    pf = p + (prev - p) * mu_ref[...]
    r = pf[:, 0:WIDTH]
    k = pf[:, WIDTH:2 * WIDTH]
    v = pf[:, 2 * WIDTH:3 * WIDTH]
    x = pf[:, 3 * WIDTH:3 * WIDTH + LANES]
    lw = _dot(jnp.tanh(x), wup_ref[...])
    la = _dot(x, aup_ref[...])
    gate = _dot(_sigmoid(x), gup_ref[...])
    w_log = -_softplus(-(w0_ref[...] + lw)) - 0.5
    a = _sigmoid(a0_ref[...] + la)
    ones_bd = _head_ones()
    kkv = k * kk_ref[...]
    kk = kkv * lax.rsqrt(_head_sum(kkv * kkv, ones_bd) + EPS)
    k = k * (1.0 + (a - 1.0) * ka_ref[...])
    b = kk * a
    ld = -jnp.exp(w_log)

    m = _sbs_masks()
    same = m["same_w"]
    cum_incl = (_iota2((CHUNK, CHUNK), 0) >= _iota2((CHUNK, CHUNK), 1)).astype(jnp.bfloat16)
    chunks = range(ts // CHUNK)
    rows = [slice(ci * CHUNK, (ci + 1) * CHUNK) for ci in chunks]
    bd = lambda t: _bd(t, same)
    lo, hi = slice(0, CHUNK), slice(CHUNK, 2 * CHUNK)
    left, right = slice(0, STACK), slice(STACK, 2 * STACK)

    lg = [_dot_01x(cum_incl, ld[r_], 3) for r_ in rows]
    ll = [x[CHUNK - 1:CHUNK] for x in lg]
    e_neg = [jnp.exp(-x) for x in lg]
    e_rest = [jnp.exp(ll[i] - lg[i]) for i in chunks]
    kt = [kk[rows[i]] * jnp.exp(lg[i] - ld[rows[i]]) for i in chunks]
    rt = [r[rows[i]] * jnp.exp(lg[i]) for i in chunks]
    kh = [k[rows[i]] * e_neg[i] for i in chunks]
    bh = [b[rows[i]] * e_neg[i] for i in chunks]
    kbar = [k[rows[i]] * e_rest[i] for i in chunks]
    bbar = [b[rows[i]] * e_rest[i] for i in chunks]
    big = [_dot_nt(jnp.concatenate([kt[i], rt[i]], axis=0),
                   jnp.concatenate([bd(bh[i]), bd(kh[i])], axis=0)) for i in chunks]
    a_kb = [jnp.where(m["strict"], x[lo, left], 0.0) for x in big]
    a_kk = [jnp.where(m["strict"], x[lo, right], 0.0) for x in big]
    a_rb = [jnp.where(m["incl"], x[hi, left], 0.0) for x in big]
    a_rk = [jnp.where(m["incl"], x[hi, right], 0.0) for x in big]
    t = _inv_unit_lower(a_kb, m)
    akv = [_dot(jnp.concatenate([a_kk[i], a_rk[i]], axis=0), bd(v[rows[i]])) for i in chunks]
    tk = [_dot(t[i], jnp.concatenate([bd(akv[i][lo]), bd(kt[i])], axis=1)) for i in chunks]
    tb = [_dot_tn(tk[i], bbar[i]) for i in chunks]
    ub = [_diag_blocks(x[left]) for x in tb]
    wb = [bd(_diag_blocks(x[right])) for x in tb]
    vk = [_diag_blocks(_dot_tn(v[rows[i]], kbar[i])) for i in chunks]
    wr = [jnp.concatenate([tk[i][:, right], rt[i]], axis=0) for i in chunks]

    state = st_ref[...]
    ws, outs = [], []

    def finish(i):
        u = ws[i][lo] + tk[i][:, left]
        outs.append(ws[i][hi] + akv[i][hi] - _dot(a_rb[i], bd(u)))

    for i in chunks:
        prev = state
        state = state * jnp.exp(ll[i]) - _dot(state, wb[i]) + (vk[i] - ub[i])
        ws.append(_dot_nt(wr[i], bd(prev)))
        if i > 0:
            finish(i - 1)
    finish(chunks[-1])
    st_ref[...] = state
    y = jnp.concatenate(outs, axis=0)
    mean =_head_sum(y, ones_bd) * (1.0 / HEAD_DIM)
    yc = y - mean
    var = _head_sum(yc * yc, ones_bd) * (1.0 / HEAD_DIM)
    yn = yc * lax.rsqrt(var + RW_LN_EPS) * lng_ref[...] + lnb_ref[...]
    bonus = _head_sum(r * k * rk_ref[...], ones_bd) * v
    o_ref[...] = (yn + bonus) * gate


def _rw(p_rw, prm, l):
    B, S, n_rw = p_rw.shape
    ts = min(TS_REC, S)
    vec = lambda n=WIDTH: pl.BlockSpec((None, 1, n), lambda b, s: (l, 0, 0))
    lora = lambda: pl.BlockSpec((None, LANES, WIDTH), lambda b, s: (l, 0, 0))
    return pl.pallas_call(
        functools.partial(_rw_kernel, ts=ts),
        grid=(B, S // ts),
        in_specs=[
            pl.BlockSpec((None, ts, n_rw), lambda b, s: (b, s, 0)),
            vec(n_rw), vec(), lora(), vec(), lora(), lora(), vec(), vec(), vec(), vec(), vec(),
        ],
        out_specs=pl.BlockSpec((None, ts, WIDTH), lambda b, s: (b, s, 0)),
        out_shape=jax.ShapeDtypeStruct((B, S, WIDTH), F32),
        scratch_shapes=[pltpu.VMEM((SUBLANES, n_rw), F32), pltpu.VMEM((CHUNK, WIDTH), F32)],
        compiler_params=pltpu.CompilerParams(
            dimension_semantics=("parallel", "arbitrary"), vmem_limit_bytes=VMEM_LIMIT),
        name="rwkv7",
    )(p_rw, prm["rw_mu"], prm["rw_w0"], prm["rw_wup"], prm["rw_a0"], prm["rw_aup"], prm["rw_gup"],
      prm["rw_k_k"], prm["rw_k_a"], prm["rw_r_k"], prm["rw_ln_g"], prm["rw_ln_b"])


def _out_kernel(x_ref, mod_ref, g_ref, o0_ref, o1_ref, o2_ref, o3_ref,
                wg_ref, bg_ref, wb_ref, wo_ref, xo_ref, *, d):
    x = x_ref[...]
    mod = mod_ref[...]
    h = _modulated_norm(x, g_ref[...], mod[:, d:2 * d], mod[:, 0:d]).astype(MXU_DTYPE)
    mixed = None
    for n, o_ref in enumerate((o0_ref, o1_ref, o2_ref, o3_ref)):
        gate = _sigmoid(jnp.dot(h, wg_ref[n], preferred_element_type=F32) + bg_ref[n])
        y = jnp.dot(o_ref[...].astype(MXU_DTYPE), wb_ref[n], preferred_element_type=F32)
        mixed = gate * y if mixed is None else mixed + gate * y
    out = jnp.dot(mixed.astype(MXU_DTYPE), wo_ref[...], preferred_element_type=F32)
    xo_ref[...] = x + mod[:, 2 * d:3 * d] * out


def _out_proj(x, mod_l, norm_g, branches, prm, l):
    B, S, D = x.shape
    tm = min(TM_DENSE, S)
    tok = lambda n: pl.BlockSpec((None, tm, n), lambda b, s: (b, s, 0))
    const = lambda shape: pl.BlockSpec((None,) + shape, lambda b, s: (l,) + (0,) * len(shape))
    return pl.pallas_call(
        functools.partial(_out_kernel, d=D),
        grid=(B, S // tm),
        in_specs=[
            tok(D),
            pl.BlockSpec((None, 1, 6 * D), lambda b, s: (b, 0, 0)),
            const((1, D)),
            tok(WIDTH), tok(WIDTH), tok(WIDTH), tok(WIDTH),
            const((N_HEADS, D, D)), const((N_HEADS, 1, D)), const((N_HEADS, WIDTH, D)), const((D, D)),
        ],
        out_specs=tok(D),
        out_shape=jax.ShapeDtypeStruct((B, S, D), F32),
        compiler_params=pltpu.CompilerParams(
            dimension_semantics=("parallel", "parallel"), vmem_limit_bytes=VMEM_LIMIT),
        name="out_proj",
    )(x, mod_l, norm_g, *branches, prm["w_gate"], prm["b_gate"], prm["w_branch"], prm["w_out"])


def _ffn_kernel(x_ref, mod_ref, g_ref, wg_ref, wu_ref, cw_ref, wd_ref, fg_ref, xo_ref, halo_ref,
                *, d, tm, fc, final):
    @pl.when(pl.program_id(1) == 0)
    def _():
        halo_ref[...] = jnp.zeros_like(halo_ref)

    x = x_ref[...]
    mod = mod_ref[...]
    h = _modulated_norm(x, g_ref[...], mod[:, 4 * d:5 * d], mod[:, 3 * d:4 * d]).astype(MXU_DTYPE)
    acc = None
    for j in range(FFN_SPLIT):
        cols = slice(j * fc, (j + 1) * fc)
        a_pre = jnp.dot(h, wg_ref[:, cols], preferred_element_type=F32)
        up = jnp.dot(h, wu_ref[:, cols], preferred_element_type=F32)
        halo = halo_ref[:, cols]
        cw = cw_ref[:, cols]
        a = (cw[2:3] * a_pre + cw[1:2] * _shift_rows(a_pre, 1, halo)
             + cw[0:1] * _shift_rows(a_pre, 2, halo))
        halo_ref[:, cols] = a_pre[tm - SUBLANES:tm]
        act = (_gelu_tanh(a) * up).astype(MXU_DTYPE)
        part = jnp.dot(act, wd_ref[cols, :], preferred_element_type=F32)
        acc = part if acc is None else acc + part
    y = x + mod[:, 5 * d:6 * d] * acc
    if final:
        y = y * lax.rsqrt(jnp.mean(y * y, axis=-1, keepdims=True) + EPS) * fg_ref[...]
    xo_ref[...] = y


def _ffn(x, mod_l, norm_g, prm, final_g, l, final):
    B, S, D = x.shape
    F = prm["ffn_w_gate"].shape[-1]
    tm = min(TM_DENSE, S)
    fc = F // FFN_SPLIT
    tok = lambda n: pl.BlockSpec((None, tm, n), lambda b, s: (b, s, 0))
    const = lambda shape: pl.BlockSpec((None,) + shape, lambda b, s: (l,) + (0,) * len(shape))
    return pl.pallas_call(
        functools.partial(_ffn_kernel, d=D, tm=tm, fc=fc, final=final),
        grid=(B, S // tm),
        in_specs=[
            tok(D),
            pl.BlockSpec((None, 1, 6 * D), lambda b, s: (b, 0, 0)),
            const((1, D)),
            const((D, F)), const((D, F)), const((3, F)), const((F, D)),
            pl.BlockSpec((1, D), lambda b, s: (0, 0)),
        ],
        out_specs=tok(D),
        out_shape=jax.ShapeDtypeStruct((B, S, D), F32),
        scratch_shapes=[pltpu.VMEM((SUBLANES, F), F32)],
        compiler_params=pltpu.CompilerParams(
            dimension_semantics=("parallel", "arbitrary"), vmem_limit_bytes=VMEM_LIMIT),
        name="conv_ffn",
    )(x, mod_l, norm_g, prm["ffn_w_gate"], prm["ffn_w_up"], prm["ffn_conv_w"], prm["ffn_w_down"],
      final_g)


def _prepare(w_in, lru_conv_w, lru_conv_b, lru_w_r, lru_b_r, lru_w_i, lru_b_i, lru_lambda,
             gdn_conv_w, gdn_a_log, gdn_dt_bias, gdn_norm_g,
             rw_mu, rw_w0, rw_w_up, rw_a0, rw_a_up, rw_g_up, rw_k_k, rw_k_a, rw_r_k, rw_ln_g, rw_ln_b,
             w_branch, w_gate, b_gate, w_out, ffn_w_gate, ffn_w_up, ffn_conv_w, ffn_w_down):
    L, D, _ = w_in.shape
    bf = lambda t: t.astype(MXU_DTYPE)
    row = lambda t: t.reshape(L, 1, -1)

    n_ab = 2 * N_HEADS
    c_ab = 512 + 768 + 1024
    w_in_p = jnp.concatenate(
        [w_in[:, :, :c_ab], w_in[:, :, c_ab:c_ab + n_ab],
         jnp.zeros((L, D, LANES - n_ab), w_in.dtype), w_in[:, :, c_ab + n_ab:]], axis=-1)

    def block_diag(w):
        eye = jnp.eye(N_HEADS, dtype=w.dtype)
        return jnp.einsum("lnef,nm->lnemf", w, eye).reshape(L, WIDTH, WIDTH)

    def lane_pad(t, lo):
        return jnp.pad(t, ((0, 0), (lo, LANES - lo - t.shape[1]))).reshape(L, 1, LANES)

    def lora_pad(w, lo):
        return jnp.pad(w, ((0, 0), (lo, LANES - lo - w.shape[1]), (0, 0)))

    return dict(
        w_in=bf(w_in_p),
        lru_conv_w=lru_conv_w, lru_conv_b=row(lru_conv_b),
        lru_wr=bf(block_diag(lru_w_r)), lru_b_r=row(lru_b_r),
        lru_wi=bf(block_diag(lru_w_i)), lru_b_i=row(lru_b_i), lru_lambda=row(lru_lambda),
        gdn_conv_w=gdn_conv_w, gdn_a_log=lane_pad(gdn_a_log, 0), gdn_dt_bias=lane_pad(gdn_dt_bias, 0),
        gdn_norm_g=row(jnp.tile(gdn_norm_g, (1, N_HEADS))),
        rw_mu=row(rw_mu), rw_w0=row(rw_w0), rw_a0=row(rw_a0),
        rw_wup=bf(lora_pad(rw_w_up, 0)), rw_aup=bf(lora_pad(rw_a_up, W_LORA)),
        rw_gup=bf(lora_pad(rw_g_up, W_LORA + A_LORA)),
        rw_k_k=row(rw_k_k), rw_k_a=row(rw_k_a), rw_r_k=row(rw_r_k),
        rw_ln_g=row(rw_ln_g), rw_ln_b=row(rw_ln_b),
        w_gate=bf(w_gate), b_gate=b_gate.reshape(L, N_HEADS, 1, D), w_branch=bf(w_branch), w_out=bf(w_out),
        ffn_w_gate=bf(ffn_w_gate), ffn_w_up=bf(ffn_w_up), ffn_conv_w=ffn_conv_w, ffn_w_down=bf(ffn_w_down),
    )


def kernel(x, c, norm1_g, norm2_g, final_g, w_ada, b_ada, w_in, lru_conv_w, lru_conv_b, lru_w_r, lru_b_r, lru_w_i, lru_b_i, lru_lambda, gdn_conv_w, gdn_a_log, gdn_dt_bias, gdn_norm_g, rw_mu, rw_w0, rw_w_up, rw_a0, rw_a_up, rw_g_up, rw_k_k, rw_k_a, rw_r_k, rw_ln_g, rw_ln_b, w_branch, w_gate, b_gate, w_out, ffn_w_gate, ffn_w_up, ffn_conv_w, ffn_w_down):
    B, S, D = x.shape
    L = w_in.shape[0]
    prm = _prepare(w_in, lru_conv_w, lru_conv_b, lru_w_r, lru_b_r, lru_w_i, lru_b_i, lru_lambda,
                   gdn_conv_w, gdn_a_log, gdn_dt_bias, gdn_norm_g,
                   rw_mu, rw_w0, rw_w_up, rw_a0, rw_a_up, rw_g_up, rw_k_k, rw_k_a, rw_r_k,
                   rw_ln_g, rw_ln_b, w_branch, w_gate, b_gate, w_out,
                   ffn_w_gate, ffn_w_up, ffn_conv_w, ffn_w_down)
    mod = _modulation(c, w_ada, b_ada).reshape(L, B, 1, 6 * D)
    n1 = norm1_g.reshape(L, 1, D)
    n2 = norm2_g.reshape(L, 1, D)
    fg = final_g.reshape(1, D)
    for l in range(L):
        o_lru, p_sb, p_gdn, p_ab, p_rw = _in_proj(x, mod[l], n1, prm, l)
        branches = (o_lru, _sb(p_sb), _gdn(p_gdn, p_ab, prm, l), _rw(p_rw, prm, l))
        x = _out_proj(x, mod[l], n1, branches, prm, l)
        x = _ffn(x, mod[l], n2, prm, fg, l, final=(l == L - 1))
    return x
```

```python
import functools

import jax
import jax.numpy as jnp
from jax import lax
from jax.experimental import pallas as pl
from jax.experimental.pallas import tpu as pltpu

F32 = jnp.float32
MXU_DTYPE = jnp.bfloat16

HEAD_DIM = 64
N_HEADS = 4
WIDTH = N_HEADS * HEAD_DIM
CHUNK = 64
STACK = N_HEADS * CHUNK
SUBLANES = 8
LANES = 128
EPS = 1e-6
RW_LN_EPS = 64e-5
LRU_C = 8.0
W_LORA, A_LORA, G_LORA = 32, 32, 64
FFN_SPLIT = 2

TM_DENSE = 1024
TS_REC = 1024
TQ_SB = 256
SB_GROUPS = (4, 2, 1)

VMEM_LIMIT = 56 * 1024 * 1024


def _dot(a, b):
    return lax.dot_general(a, b.astype(MXU_DTYPE), (((1,), (0,)), ((), ())),
                           preferred_element_type=F32)


def _dot_nt(a, b):
    return lax.dot_general(a.astype(MXU_DTYPE), b.astype(MXU_DTYPE),
                           (((1,), (1,)), ((), ())), preferred_element_type=F32)


def _dot_tn(a, b):
    return lax.dot_general(a.astype(MXU_DTYPE), b.astype(MXU_DTYPE),
                           (((0,), (0,)), ((), ())), preferred_element_type=F32)


def _split(x, parts):
    out = []
    r = x
    for _ in range(parts - 1):
        p = r.astype(jnp.bfloat16)
        out.append(p)
        r = r - p.astype(F32)
    out.append(r.astype(jnp.bfloat16))
    return out


def _dot_x01(x, m01, parts):
    acc = None
    for p in _split(x, parts):
        t = jnp.dot(p, m01, preferred_element_type=F32)
        acc = t if acc is None else acc + t
    return acc


def _dot_01x(m01, x, parts):
    acc = None
    for p in _split(x, parts):
        t = jnp.dot(m01, p, preferred_element_type=F32)
        acc = t if acc is None else acc + t
    return acc


def _sigmoid(x):
    return jax.nn.sigmoid(x)


def _softplus(x):
    return jnp.maximum(x, 0.0) + jnp.log1p(jnp.exp(-jnp.abs(x)))


def _silu(x):
    return x * _sigmoid(x)


def _gelu_tanh(x):
    return 0.5 * x * (1.0 + jnp.tanh(0.7978845608028654 * (x + 0.044715 * (x * x * x))))


def _iota2(shape, dim):
    return lax.broadcasted_iota(jnp.int32, shape, dim)


def _shift_rows(x, k, halo):
    r = pltpu.roll(x, k, 0)
    hr = pltpu.roll(halo, k, 0)
    first = jnp.where(_iota2(halo.shape, 0) < k, hr, r[0:SUBLANES])
    return jnp.concatenate([first, r[SUBLANES:]], axis=0)


def _head_ones():
    return ((_iota2((WIDTH, WIDTH), 0) >> 6) == (_iota2((WIDTH, WIDTH), 1) >> 6)).astype(jnp.bfloat16)


def _head_sum(x, ones_bd):
    return _dot_x01(x, ones_bd, 2)


def _sbs_masks():
    t = _iota2((CHUNK, WIDTH), 0)
    j = _iota2((CHUNK, WIDTH), 1) & (CHUNK - 1)
    r = _iota2((STACK, STACK), 0)
    c = _iota2((STACK, STACK), 1)
    return dict(
        incl=t >= j,
        strict=t > j,
        m16=(t >> 4) == (j >> 4),
        m32=(t >> 5) == (j >> 5),
        eye=t == j,
        same_w=((r >> 6) == (c >> 6)).astype(MXU_DTYPE),
    )


def _bd(x, same):
    return jnp.concatenate([x.astype(MXU_DTYPE)] * N_HEADS, axis=0) * same


def _diag_blocks(full):
    lane_head = _iota2((CHUNK, WIDTH), 1) >> 6
    out = full[0:CHUNK]
    for h in range(1, N_HEADS):
        out = jnp.where(lane_head == h, full[h * CHUNK:(h + 1) * CHUNK], out)
    return out


def _expand_heads(x, lane_base):
    e = (_iota2((LANES, WIDTH), 0) == (_iota2((LANES, WIDTH), 1) >> 6) + lane_base)
    return _dot_x01(x, e.astype(jnp.bfloat16), 2)


def _inv_unit_lower(mats, m):
    n = range(len(mats))
    same = m["same_w"]
    a0 = [jnp.where(m["m16"], a, 0.0) for a in mats]
    a1 = [jnp.where(m["m32"] & jnp.logical_not(m["m16"]), a, 0.0) for a in mats]
    a2 = [jnp.where(m["m32"], 0.0, a) for a in mats]
    p = [jnp.where(m["eye"], 1.0, 0.0) - a for a in a0]
    x = [_dot(a, _bd(a, same)) for a in a0]
    for _ in range(2):
        px = [_dot(jnp.concatenate([p[i], x[i]], axis=0), _bd(x[i], same)) for i in n]
        p = [p[i] + px[i][0:CHUNK] for i in n]
        x = [px[i][CHUNK:2 * CHUNK] for i in n]
    p = [p[i] + _dot(p[i], _bd(x[i], same)) for i in n]
    for off in (a1, a2):
        t = [_dot(p[i], _bd(off[i], same)) for i in n]
        p = [p[i] - _dot(t[i], _bd(p[i], same)) for i in n]
    return p


def _modulated_norm(x, g, scale, shift):
    ms = jnp.mean(x * x, axis=-1, keepdims=True)
    return (x * lax.rsqrt(ms + EPS) * g) * (1.0 + scale) + shift


def _mod_kernel(c_ref, w_ref, b_ref, o_ref):
    c = c_ref[...]
    cond = c * _sigmoid(c)
    o_ref[...] = jnp.dot(cond, w_ref[...], precision=lax.Precision.HIGHEST,
                         preferred_element_type=F32) + b_ref[...]


def _modulation(c, w_ada, b_ada):
    L, D, D6 = w_ada.shape
    B = c.shape[0]
    nj = D6 // D
    return pl.pallas_call(
        _mod_kernel,
        grid=(L, nj),
        in_specs=[
            pl.BlockSpec((B, D), lambda l, j: (0, 0)),
            pl.BlockSpec((None, D, D), lambda l, j: (l, 0, j)),
            pl.BlockSpec((None, 1, D), lambda l, j: (l, 0, j)),
        ],
        out_specs=pl.BlockSpec((None, B, D), lambda l, j: (l, 0, j)),
        out_shape=jax.ShapeDtypeStruct((L, B, D6), F32),
        name="adaln_mod",
    )(c, w_ada, b_ada.reshape(L, 1, D6))


LRU_COLS = 2 * WIDTH
IN_COLS = (("sb", 768), ("gdn", 1024), ("ab", LANES), ("rw", 896))


def _rg_lru(p, cw, cb, wr, br, wi, bi, lam, halo_ref, h_ref, ts):
    x_in = p[:, 0:WIDTH]
    y_in = p[:, WIDTH:2 * WIDTH]
    halo = halo_ref[...]
    u = (cw[3:4] * x_in + cw[2:3] * _shift_rows(x_in, 1, halo)
         + cw[1:2] * _shift_rows(x_in, 2, halo) + cw[0:1] * _shift_rows(x_in, 3, halo) + cb)
    halo_ref[...] = x_in[ts - SUBLANES:ts]

    ub = u.astype(MXU_DTYPE)
    r = _sigmoid(jnp.dot(ub, wr, preferred_element_type=F32) + br)
    i = _sigmoid(jnp.dot(ub, wi, preferred_element_type=F32) + bi)
    log_a = (-LRU_C * r) * _softplus(-lam)
    a = jnp.exp(log_a)
    bv = jnp.sqrt(-jnp.tanh(log_a) * (a * a + 1.0)) * (i * u)

    low = _iota2((SUBLANES, WIDTH), 0)
    carry = h_ref[SUBLANES - 1:SUBLANES]
    groups = []
    for g in range(ts // SUBLANES):
        ag = a[g * SUBLANES:(g + 1) * SUBLANES]
        bg = bv[g * SUBLANES:(g + 1) * SUBLANES]
        for d in (1, 2, 4):
            bg = ag * jnp.where(low < d, 0.0, pltpu.roll(bg, d, 0)) + bg
            ag = ag * jnp.where(low < d, 1.0, pltpu.roll(ag, d, 0))
        hg = bg + ag * carry
        carry = hg[SUBLANES - 1:SUBLANES]
        groups.append(hg)
    h = jnp.concatenate(groups, axis=0)
    h_ref[...] = groups[-1]
    return h * _gelu_tanh(y_in)


def _in_kernel(x_ref, mod_ref, g_ref, w_ref, cw_ref, cb_ref, wr_ref, br_ref, wi_ref, bi_ref,
               lam_ref, o_lru, *rest, d, tm):
    o_refs, (halo_ref, h_ref) = rest[:len(IN_COLS)], rest[len(IN_COLS):]

    @pl.when(pl.program_id(1) == 0)
    def _():
        halo_ref[...] = jnp.zeros_like(halo_ref)
        h_ref[...] = jnp.zeros_like(h_ref)

    mod = mod_ref[...]
    h = _modulated_norm(x_ref[...], g_ref[...], mod[:, d:2 * d], mod[:, 0:d]).astype(MXU_DTYPE)
    p_lru = jnp.dot(h, w_ref[:, 0:LRU_COLS], preferred_element_type=F32)
    o_lru[...] = _rg_lru(p_lru, cw_ref[...], cb_ref[...], wr_ref[...], br_ref[...], wi_ref[...],
                         bi_ref[...], lam_ref[...], halo_ref, h_ref, tm)
    c0 = LRU_COLS
    for (_, n), o_ref in zip(IN_COLS, o_refs):
        o_ref[...] = jnp.dot(h, w_ref[:, c0:c0 + n], preferred_element_type=F32)
        c0 += n


def _in_proj(x, mod_l, norm_g, prm, l):
    B, S, D = x.shape
    tm = min(TM_DENSE, S)
    n_in = prm["w_in"].shape[-1]
    tok = lambda n: pl.BlockSpec((None, tm, n), lambda b, s: (b, s, 0))
    vec = lambda: pl.BlockSpec((None, 1, WIDTH), lambda b, s: (l, 0, 0))
    mat = lambda r: pl.BlockSpec((None, r, WIDTH), lambda b, s: (l, 0, 0))
    return pl.pallas_call(
        functools.partial(_in_kernel, d=D, tm=tm),
        grid=(B, S // tm),
        in_specs=[
            tok(D),
            pl.BlockSpec((None, 1, 6 * D), lambda b, s: (b, 0, 0)),
            pl.BlockSpec((None, 1, D), lambda b, s: (l, 0, 0)),
            pl.BlockSpec((None, D, n_in), lambda b, s: (l, 0, 0), pipeline_mode=pl.Buffered(1)),
            mat(4), vec(), mat(WIDTH), vec(), mat(WIDTH), vec(), vec(),
        ],
        out_specs=[tok(WIDTH)] + [tok(n) for _, n in IN_COLS],
        out_shape=[jax.ShapeDtypeStruct((B, S, WIDTH), F32)]
        + [jax.ShapeDtypeStruct((B, S, n), F32) for _, n in IN_COLS],
        scratch_shapes=[pltpu.VMEM((SUBLANES, WIDTH), F32), pltpu.VMEM((SUBLANES, WIDTH), F32)],
        compiler_params=pltpu.CompilerParams(
            dimension_semantics=("parallel", "arbitrary"), vmem_limit_bytes=VMEM_LIMIT),
        name="in_proj_lru",
    )(x, mod_l, norm_g, prm["w_in"], prm["lru_conv_w"], prm["lru_conv_b"], prm["lru_wr"],
      prm["lru_b_r"], prm["lru_wi"], prm["lru_b_i"], prm["lru_lambda"])


def _sb_kernel(q_ref, k_ref, v_ref, o_ref, acc_ref, *, tq):
    i = pl.program_id(1)
    q = q_ref[...] * (HEAD_DIM ** -0.5)
    head = _iota2((1, WIDTH), 1) >> 6
    hm = [head == h for h in range(N_HEADS)]
    qh = [jnp.where(hm[h], q, 0.0).astype(MXU_DTYPE) for h in range(N_HEADS)]
    row = _iota2((tq, tq), 0)
    col = _iota2((tq, tq), 1)
    rev_incl = (row >= col).astype(jnp.bfloat16)
    past = col < row
    acc_ref[...] = jnp.zeros_like(acc_ref)

    def tiles(first, count, carries, diag):
        heads = range(N_HEADS)
        group = range(count)
        units = [(g, h) for g in group for h in heads]
        starts = [pl.multiple_of((first - g) * tq, tq) for g in group]
        kts = [k_ref[pl.ds(s, tq), :].astype(MXU_DTYPE) for s in starts]
        vts = [v_ref[pl.ds(s, tq), :].astype(MXU_DTYPE) for s in starts]
        zs = {(g, h): lax.dot_general(qh[h], kts[g], (((1,), (1,)), ((), ())),
                                      preferred_element_type=F32) for g, h in units}
        sps = {u: jnp.maximum(zs[u], 0.0) + jnp.log(1.0 + jnp.exp(-jnp.abs(zs[u]))) for u in units}
        if diag:
            sps = {u: jnp.where(past, sps[u], 0.0) for u in units}
        part = {u: _dot_x01(sps[u], rev_incl, 2) for u in units}
        css = {}
        new = []
        for h in heads:
            c = carries[h]
            for g in group:
                css[(g, h)] = part[(g, h)] + c
                c = css[(g, h)][:, 0:1]
            new.append(c)
        ws = {u: jnp.exp(zs[u] - css[u]) for u in units}
        if diag:
            ws = {u: jnp.where(past, ws[u], 0.0) for u in units}
        total = None
        for g in group:
            pv = None
            for h in heads:
                o = jnp.dot(ws[(g, h)].astype(MXU_DTYPE), vts[g], preferred_element_type=F32)
                pv = o if pv is None else jnp.where(hm[h], o, pv)
            total = pv if total is None else total + pv
        acc_ref[...] += total
        return tuple(new)

    zero = jnp.zeros((tq, 1), F32)
    carries = tiles(i, 1, (zero,) * N_HEADS, True)
    first, rest = i - 1, i
    for g in SB_GROUPS:
        n = rest // g
        carries = lax.fori_loop(
            0, n, lambda it, c, first=first, g=g: tiles(first - it * g, g, c, False), carries)
        first, rest = first - n * g, rest - n * g
    o_ref[...] = acc_ref[...]


def _sb(p_sb):
    B, S, _ = p_sb.shape
    tq = min(TQ_SB, S)
    return pl.pallas_call(
        functools.partial(_sb_kernel, tq=tq),
        grid=(B, S // tq),
        in_specs=[
            pl.BlockSpec((None, tq, WIDTH), lambda b, i: (b, i, 0)),
            pl.BlockSpec((None, S, WIDTH), lambda b, i: (b, 0, 1)),
            pl.BlockSpec((None, S, WIDTH), lambda b, i: (b, 0, 2)),
        ],
        out_specs=pl.BlockSpec((None, tq, WIDTH), lambda b, i: (b, i, 0)),
        out_shape=jax.ShapeDtypeStruct((B, S, WIDTH), F32),
        scratch_shapes=[pltpu.VMEM((tq, WIDTH), F32)],
        compiler_params=pltpu.CompilerParams(
            dimension_semantics=("parallel", "arbitrary"), vmem_limit_bytes=VMEM_LIMIT),
        name="stick_breaking",
    )(p_sb, p_sb, p_sb)


def _gdn_kernel(p_ref, ab_ref, cw_ref, alog_ref, dtb_ref, ng_ref, o_ref,
                halo_ref, st_ref, *, ts):
    @pl.when(pl.program_id(1) == 0)
    def _():
        halo_ref[...] = jnp.zeros_like(halo_ref)
        st_ref[...] = jnp.zeros_like(st_ref)

    qkv = p_ref[:, 0:3 * WIDTH]
    halo = halo_ref[...]
    cw = cw_ref[...]
    c = (cw[3:4] * qkv + cw[2:3] * _shift_rows(qkv, 1, halo)
         + cw[1:2] * _shift_rows(qkv, 2, halo) + cw[0:1] * _shift_rows(qkv, 3, halo))
    halo_ref[...] = qkv[ts - SUBLANES:ts]
    c = _silu(c)
    ones_bd = _head_ones()
    q = c[:, 0:WIDTH]
    k = c[:, WIDTH:2 * WIDTH]
    qn = q * lax.rsqrt(_head_sum(q * q, ones_bd) + EPS) * (HEAD_DIM ** -0.5)
    kn = k * lax.rsqrt(_head_sum(k * k, ones_bd) + EPS)
    v = c[:, 2 * WIDTH:3 * WIDTH]
    ab = ab_ref[...]
    g = -jnp.exp(alog_ref[...]) * _softplus(ab + dtb_ref[...])
    beta = _sigmoid(ab)

    gexp = _expand_heads(g, 0)
    bx = _expand_heads(beta, N_HEADS)

    m = _sbs_masks()
    same = m["same_w"]
    cum_incl = (_iota2((CHUNK, CHUNK), 0) >= _iota2((CHUNK, CHUNK), 1)).astype(jnp.bfloat16)
    ones_cc = jnp.ones((CHUNK, CHUNK), jnp.bfloat16)
    chunks = range(ts // CHUNK)
    rows = [slice(ci * CHUNK, (ci + 1) * CHUNK) for ci in chunks]
    lo, hi = slice(0, CHUNK), slice(CHUNK, 2 * CHUNK)
    left, right = slice(0, STACK), slice(STACK, 2 * STACK)

    gx = [_dot_01x(cum_incl, gexp[r], 3) for r in rows]
    g_row = [_dot_01x(ones_cc, jnp.where(m["eye"], x, 0.0), 3) for x in gx]
    decay = [jnp.exp(jnp.where(m["incl"], gx[i] - g_row[i], -1e30)) for i in chunks]
    k_beta = [kn[rows[i]] * bx[rows[i]] for i in chunks]
    kq = [_dot_nt(jnp.concatenate([k_beta[i], qn[rows[i]]], axis=0), _bd(kn[rows[i]], same))
          for i in chunks]
    a = [jnp.where(m["strict"], kq[i][lo] * decay[i], 0.0) for i in chunks]
    qk = [jnp.where(m["incl"], kq[i][hi] * decay[i], 0.0) for i in chunks]
    t = _inv_unit_lower(a, m)
    eg = [jnp.exp(x) for x in gx]
    sol = [_dot(t[i], jnp.concatenate([_bd(v[rows[i]] * bx[rows[i]], same),
                                       _bd(k_beta[i] * eg[i], same)], axis=1))
           for i in chunks]
    gl = [x[CHUNK - 1:CHUNK] for x in gx]
    k_dec = [kn[rows[i]] * jnp.exp(gl[i] - gx[i]) for i in chunks]
    nm = [_dot_tn(sol[i], k_dec[i]) for i in chunks]
    ut_kd = [_diag_blocks(x[left]) for x in nm]
    wt_kd = [_bd(_diag_blocks(x[right]), same) for x in nm]
    wq = [jnp.concatenate([sol[i][:, right], qn[rows[i]] * eg[i]], axis=0) for i in chunks]

    state = st_ref[...]
    ws, outs = [], []

    def finish(i):
        u = sol[i][:, left] - ws[i][lo]
        outs.append(ws[i][hi] + _dot(qk[i], _bd(u, same)))

    for i in chunks:
        prev = state
        state = state * jnp.exp(gl[i]) - _dot(state, wt_kd[i]) + ut_kd[i]
        ws.append(_dot_nt(wq[i], _bd(prev, same)))
        if i > 0:
            finish(i - 1)
    finish(chunks[-1])
    st_ref[...] = state
    o = jnp.concatenate(outs, axis=0)
    o = o * lax.rsqrt(_head_sum(o * o, ones_bd) * (1.0 / HEAD_DIM) + EPS) * ng_ref[...]
    o_ref[...] = o * _silu(p_ref[:, 3 * WIDTH:4 * WIDTH])


def _gdn(p_gdn, p_ab, prm, l):
    B, S, _ = p_gdn.shape
    ts = min(TS_REC, S)
    return pl.pallas_call(
        functools.partial(_gdn_kernel, ts=ts),
        grid=(B, S // ts),
        in_specs=[
            pl.BlockSpec((None, ts, 4 * WIDTH), lambda b, s: (b, s, 0)),
            pl.BlockSpec((None, ts, LANES), lambda b, s: (b, s, 0)),
            pl.BlockSpec((None, 4, 3 * WIDTH), lambda b, s: (l, 0, 0)),
            pl.BlockSpec((None, 1, LANES), lambda b, s: (l, 0, 0)),
            pl.BlockSpec((None, 1, LANES), lambda b, s: (l, 0, 0)),
            pl.BlockSpec((None, 1, WIDTH), lambda b, s: (l, 0, 0)),
        ],
        out_specs=pl.BlockSpec((None, ts, WIDTH), lambda b, s: (b, s, 0)),
        out_shape=jax.ShapeDtypeStruct((B, S, WIDTH), F32),
        scratch_shapes=[pltpu.VMEM((SUBLANES, 3 * WIDTH), F32), pltpu.VMEM((CHUNK, WIDTH), F32)],
        compiler_params=pltpu.CompilerParams(
            dimension_semantics=("parallel", "arbitrary"), vmem_limit_bytes=VMEM_LIMIT),
        name="gated_deltanet",
    )(p_gdn, p_ab, prm["gdn_conv_w"], prm["gdn_a_log"], prm["gdn_dt_bias"], prm["gdn_norm_g"])


def _rw_kernel(p_ref, mu_ref, w0_ref, wup_ref, a0_ref, aup_ref, gup_ref, kk_ref, ka_ref, rk_ref,
               lng_ref, lnb_ref, o_ref,
               halo_ref, st_ref, *, ts):
    @pl.when(pl.program_id(1) == 0)
    def _():
        halo_ref[...] = jnp.zeros_like(halo_ref)
        st_ref[...] = jnp.zeros_like(st_ref)

    p = p_ref[...]
    prev = _shift_rows(p, 1, halo_ref[...])
    halo_ref[...] = p[ts - SUBLANES:ts]
    pf = p + (prev - p) * mu_ref[...]
    r = pf[:, 0:WIDTH]
    k = pf[:, WIDTH:2 * WIDTH]
    v = pf[:, 2 * WIDTH:3 * WIDTH]
    x = pf[:, 3 * WIDTH:3 * WIDTH + LANES]
    lw = _dot(jnp.tanh(x), wup_ref[...])
    la = _dot(x, aup_ref[...])
    gate = _dot(_sigmoid(x), gup_ref[...])
    w_log = -_softplus(-(w0_ref[...] + lw)) - 0.5
    a = _sigmoid(a0_ref[...] + la)
    ones_bd = _head_ones()
    kkv = k * kk_ref[...]
    kk = kkv * lax.rsqrt(_head_sum(kkv * kkv, ones_bd) + EPS)
    k = k * (1.0 + (a - 1.0) * ka_ref[...])
    b = kk * a
    ld = -jnp.exp(w_log)

    m = _sbs_masks()
    same = m["same_w"]
    cum_incl = (_iota2((CHUNK, CHUNK), 0) >= _iota2((CHUNK, CHUNK), 1)).astype(jnp.bfloat16)
    chunks = range(ts // CHUNK)
    rows = [slice(ci * CHUNK, (ci + 1) * CHUNK) for ci in chunks]
    bd = lambda t: _bd(t, same)
    lo, hi = slice(0, CHUNK), slice(CHUNK, 2 * CHUNK)
    left, right = slice(0, STACK), slice(STACK, 2 * STACK)

    lg = [_dot_01x(cum_incl, ld[r_], 2) for r_ in rows]
    ll = [x[CHUNK - 1:CHUNK] for x in lg]
    e_neg = [jnp.exp(-x) for x in lg]
    e_rest = [jnp.exp(ll[i] - lg[i]) for i in chunks]
    kt = [kk[rows[i]] * jnp.exp(lg[i] - ld[rows[i]]) for i in chunks]
    rt = [r[rows[i]] * jnp.exp(lg[i]) for i in chunks]
    kh = [k[rows[i]] * e_neg[i] for i in chunks]
    bh = [b[rows[i]] * e_neg[i] for i in chunks]
    kbar = [k[rows[i]] * e_rest[i] for i in chunks]
    bbar = [b[rows[i]] * e_rest[i] for i in chunks]
    big = [_dot_nt(jnp.concatenate([kt[i], rt[i]], axis=0),
                   jnp.concatenate([bd(bh[i]), bd(kh[i])], axis=0)) for i in chunks]
    a_kb = [jnp.where(m["strict"], x[lo, left], 0.0) for x in big]
    a_kk = [jnp.where(m["strict"], x[lo, right], 0.0) for x in big]
    a_rb = [jnp.where(m["incl"], x[hi, left], 0.0) for x in big]
    a_rk = [jnp.where(m["incl"], x[hi, right], 0.0) for x in big]
    t = _inv_unit_lower(a_kb, m)
    akv = [_dot(jnp.concatenate([a_kk[i], a_rk[i]], axis=0), bd(v[rows[i]])) for i in chunks]
    tk = [_dot(t[i], jnp.concatenate([bd(akv[i][lo]), bd(kt[i])], axis=1)) for i in chunks]
    tb = [_dot_tn(tk[i], bbar[i]) for i in chunks]
    ub = [_diag_blocks(x[left]) for x in tb]
    wb = [bd(_diag_blocks(x[right])) for x in tb]
    vk = [_diag_blocks(_dot_tn(v[rows[i]], kbar[i])) for i in chunks]
    wr = [jnp.concatenate([tk[i][:, right], rt[i]], axis=0) for i in chunks]

    state = st_ref[...]
    ws, outs = [], []

    def finish(i):
        u = ws[i][lo] + tk[i][:, left]
        outs.append(ws[i][hi] + akv[i][hi] - _dot(a_rb[i], bd(u)))

    for i in chunks:
        prev = state
        state = state * jnp.exp(ll[i]) - _dot(state, wb[i]) + (vk[i] - ub[i])
        ws.append(_dot_nt(wr[i], bd(prev)))
        if i > 0:
            finish(i - 1)
    finish(chunks[-1])
    st_ref[...] = state
    y = jnp.concatenate(outs, axis=0)
    mean =_head_sum(y, ones_bd) * (1.0 / HEAD_DIM)
    yc = y - mean
    var = _head_sum(yc * yc, ones_bd) * (1.0 / HEAD_DIM)
    yn = yc * lax.rsqrt(var + RW_LN_EPS) * lng_ref[...] + lnb_ref[...]
    bonus = _head_sum(r * k * rk_ref[...], ones_bd) * v
    o_ref[...] = (yn + bonus) * gate


def _rw(p_rw, prm, l):
    B, S, n_rw = p_rw.shape
    ts = min(TS_REC, S)
    vec = lambda n=WIDTH: pl.BlockSpec((None, 1, n), lambda b, s: (l, 0, 0))
    lora = lambda: pl.BlockSpec((None, LANES, WIDTH), lambda b, s: (l, 0, 0))
    return pl.pallas_call(
        functools.partial(_rw_kernel, ts=ts),
        grid=(B, S // ts),
        in_specs=[
            pl.BlockSpec((None, ts, n_rw), lambda b, s: (b, s, 0)),
            vec(n_rw), vec(), lora(), vec(), lora(), lora(), vec(), vec(), vec(), vec(), vec(),
        ],
        out_specs=pl.BlockSpec((None, ts, WIDTH), lambda b, s: (b, s, 0)),
        out_shape=jax.ShapeDtypeStruct((B, S, WIDTH), F32),
        scratch_shapes=[pltpu.VMEM((SUBLANES, n_rw), F32), pltpu.VMEM((CHUNK, WIDTH), F32)],
        compiler_params=pltpu.CompilerParams(
            dimension_semantics=("parallel", "arbitrary"), vmem_limit_bytes=VMEM_LIMIT),
        name="rwkv7",
    )(p_rw, prm["rw_mu"], prm["rw_w0"], prm["rw_wup"], prm["rw_a0"], prm["rw_aup"], prm["rw_gup"],
      prm["rw_k_k"], prm["rw_k_a"], prm["rw_r_k"], prm["rw_ln_g"], prm["rw_ln_b"])


def _out_kernel(x_ref, mod_ref, g_ref, o0_ref, o1_ref, o2_ref, o3_ref,
                wg_ref, bg_ref, wb_ref, wo_ref, xo_ref, *, d):
    x = x_ref[...]
    mod = mod_ref[...]
    h = _modulated_norm(x, g_ref[...], mod[:, d:2 * d], mod[:, 0:d]).astype(MXU_DTYPE)
    mixed = None
    for n, o_ref in enumerate((o0_ref, o1_ref, o2_ref, o3_ref)):
        gate = _sigmoid(jnp.dot(h, wg_ref[n], preferred_element_type=F32) + bg_ref[n])
        y = jnp.dot(o_ref[...].astype(MXU_DTYPE), wb_ref[n], preferred_element_type=F32)
        mixed = gate * y if mixed is None else mixed + gate * y
    out = jnp.dot(mixed.astype(MXU_DTYPE), wo_ref[...], preferred_element_type=F32)
    xo_ref[...] = x + mod[:, 2 * d:3 * d] * out


def _out_proj(x, mod_l, norm_g, branches, prm, l):
    B, S, D = x.shape
    tm = min(TM_DENSE, S)
    tok = lambda n: pl.BlockSpec((None, tm, n), lambda b, s: (b, s, 0))
    const = lambda shape: pl.BlockSpec((None,) + shape, lambda b, s: (l,) + (0,) * len(shape),
                                       pipeline_mode=pl.Buffered(1))
    return pl.pallas_call(
        functools.partial(_out_kernel, d=D),
        grid=(B, S // tm),
        in_specs=[
            tok(D),
            pl.BlockSpec((None, 1, 6 * D), lambda b, s: (b, 0, 0)),
            const((1, D)),
            tok(WIDTH), tok(WIDTH), tok(WIDTH), tok(WIDTH),
            const((N_HEADS, D, D)), const((N_HEADS, 1, D)), const((N_HEADS, WIDTH, D)), const((D, D)),
        ],
        out_specs=tok(D),
        out_shape=jax.ShapeDtypeStruct((B, S, D), F32),
        compiler_params=pltpu.CompilerParams(
            dimension_semantics=("parallel", "parallel"), vmem_limit_bytes=VMEM_LIMIT),
        name="out_proj",
    )(x, mod_l, norm_g, *branches, prm["w_gate"], prm["b_gate"], prm["w_branch"], prm["w_out"])


def _ffn_kernel(x_ref, mod_ref, g_ref, wg_ref, wu_ref, cw_ref, wd_ref, fg_ref, xo_ref, halo_ref,
                *, d, tm, fc, final):
    @pl.when(pl.program_id(1) == 0)
    def _():
        halo_ref[...] = jnp.zeros_like(halo_ref)

    x = x_ref[...]
    mod = mod_ref[...]
    h = _modulated_norm(x, g_ref[...], mod[:, 4 * d:5 * d], mod[:, 3 * d:4 * d]).astype(MXU_DTYPE)
    acc = None
    for j in range(FFN_SPLIT):
        cols = slice(j * fc, (j + 1) * fc)
        a_pre = jnp.dot(h, wg_ref[:, cols], preferred_element_type=F32)
        up = jnp.dot(h, wu_ref[:, cols], preferred_element_type=F32)
        halo = halo_ref[:, cols]
        cw = cw_ref[:, cols]
        a = (cw[2:3] * a_pre + cw[1:2] * _shift_rows(a_pre, 1, halo)
             + cw[0:1] * _shift_rows(a_pre, 2, halo))
        halo_ref[:, cols] = a_pre[tm - SUBLANES:tm]
        act = (_gelu_tanh(a) * up).astype(MXU_DTYPE)
        part = jnp.dot(act, wd_ref[cols, :], preferred_element_type=F32)
        acc = part if acc is None else acc + part
    y = x + mod[:, 5 * d:6 * d] * acc
    if final:
        y = y * lax.rsqrt(jnp.mean(y * y, axis=-1, keepdims=True) + EPS) * fg_ref[...]
    xo_ref[...] = y


def _ffn(x, mod_l, norm_g, prm, final_g, l, final):
    B, S, D = x.shape
    F = prm["ffn_w_gate"].shape[-1]
    tm = min(TM_DENSE, S)
    fc = F // FFN_SPLIT
    tok = lambda n: pl.BlockSpec((None, tm, n), lambda b, s: (b, s, 0))
    const = lambda shape: pl.BlockSpec((None,) + shape, lambda b, s: (l,) + (0,) * len(shape),
                                       pipeline_mode=pl.Buffered(1))
    return pl.pallas_call(
        functools.partial(_ffn_kernel, d=D, tm=tm, fc=fc, final=final),
        grid=(B, S // tm),
        in_specs=[
            tok(D),
            pl.BlockSpec((None, 1, 6 * D), lambda b, s: (b, 0, 0)),
            const((1, D)),
            const((D, F)), const((D, F)), const((3, F)), const((F, D)),
            pl.BlockSpec((1, D), lambda b, s: (0, 0)),
        ],
        out_specs=tok(D),
        out_shape=jax.ShapeDtypeStruct((B, S, D), F32),
        scratch_shapes=[pltpu.VMEM((SUBLANES, F), F32)],
        compiler_params=pltpu.CompilerParams(
            dimension_semantics=("parallel", "arbitrary"), vmem_limit_bytes=VMEM_LIMIT),
        name="conv_ffn",
    )(x, mod_l, norm_g, prm["ffn_w_gate"], prm["ffn_w_up"], prm["ffn_conv_w"], prm["ffn_w_down"],
      final_g)


def _prepare(w_in, lru_conv_w, lru_conv_b, lru_w_r, lru_b_r, lru_w_i, lru_b_i, lru_lambda,
             gdn_conv_w, gdn_a_log, gdn_dt_bias, gdn_norm_g,
             rw_mu, rw_w0, rw_w_up, rw_a0, rw_a_up, rw_g_up, rw_k_k, rw_k_a, rw_r_k, rw_ln_g, rw_ln_b,
             w_branch, w_gate, b_gate, w_out, ffn_w_gate, ffn_w_up, ffn_conv_w, ffn_w_down):
    L, D, _ = w_in.shape
    bf = lambda t: t.astype(MXU_DTYPE)
    row = lambda t: t.reshape(L, 1, -1)

    n_ab = 2 * N_HEADS
    c_ab = 512 + 768 + 1024
    w_in_p = jnp.concatenate(
        [w_in[:, :, :c_ab], w_in[:, :, c_ab:c_ab + n_ab],
         jnp.zeros((L, D, LANES - n_ab), w_in.dtype), w_in[:, :, c_ab + n_ab:]], axis=-1)

    def block_diag(w):
        eye = jnp.eye(N_HEADS, dtype=w.dtype)
        return jnp.einsum("lnef,nm->lnemf", w, eye).reshape(L, WIDTH, WIDTH)

    def lane_pad(t, lo):
        return jnp.pad(t, ((0, 0), (lo, LANES - lo - t.shape[1]))).reshape(L, 1, LANES)

    def lora_pad(w, lo):
        return jnp.pad(w, ((0, 0), (lo, LANES - lo - w.shape[1]), (0, 0)))

    return dict(
        w_in=bf(w_in_p),
        lru_conv_w=lru_conv_w, lru_conv_b=row(lru_conv_b),
        lru_wr=bf(block_diag(lru_w_r)), lru_b_r=row(lru_b_r),
        lru_wi=bf(block_diag(lru_w_i)), lru_b_i=row(lru_b_i), lru_lambda=row(lru_lambda),
        gdn_conv_w=gdn_conv_w, gdn_a_log=lane_pad(gdn_a_log, 0), gdn_dt_bias=lane_pad(gdn_dt_bias, 0),
        gdn_norm_g=row(jnp.tile(gdn_norm_g, (1, N_HEADS))),
        rw_mu=row(rw_mu), rw_w0=row(rw_w0), rw_a0=row(rw_a0),
        rw_wup=bf(lora_pad(rw_w_up, 0)), rw_aup=bf(lora_pad(rw_a_up, W_LORA)),
        rw_gup=bf(lora_pad(rw_g_up, W_LORA + A_LORA)),
        rw_k_k=row(rw_k_k), rw_k_a=row(rw_k_a), rw_r_k=row(rw_r_k),
        rw_ln_g=row(rw_ln_g), rw_ln_b=row(rw_ln_b),
        w_gate=bf(w_gate), b_gate=b_gate.reshape(L, N_HEADS, 1, D), w_branch=bf(w_branch), w_out=bf(w_out),
        ffn_w_gate=bf(ffn_w_gate), ffn_w_up=bf(ffn_w_up), ffn_conv_w=ffn_conv_w, ffn_w_down=bf(ffn_w_down),
    )


def kernel(x, c, norm1_g, norm2_g, final_g, w_ada, b_ada, w_in, lru_conv_w, lru_conv_b, lru_w_r, lru_b_r, lru_w_i, lru_b_i, lru_lambda, gdn_conv_w, gdn_a_log, gdn_dt_bias, gdn_norm_g, rw_mu, rw_w0, rw_w_up, rw_a0, rw_a_up, rw_g_up, rw_k_k, rw_k_a, rw_r_k, rw_ln_g, rw_ln_b, w_branch, w_gate, b_gate, w_out, ffn_w_gate, ffn_w_up, ffn_conv_w, ffn_w_down):
    B, S, D = x.shape
    L = w_in.shape[0]
    prm = _prepare(w_in, lru_conv_w, lru_conv_b, lru_w_r, lru_b_r, lru_w_i, lru_b_i, lru_lambda,
                   gdn_conv_w, gdn_a_log, gdn_dt_bias, gdn_norm_g,
                   rw_mu, rw_w0, rw_w_up, rw_a0, rw_a_up, rw_g_up, rw_k_k, rw_k_a, rw_r_k,
                   rw_ln_g, rw_ln_b, w_branch, w_gate, b_gate, w_out,
                   ffn_w_gate, ffn_w_up, ffn_conv_w, ffn_w_down)
    mod = _modulation(c, w_ada, b_ada).reshape(L, B, 1, 6 * D)
    n1 = norm1_g.reshape(L, 1, D)
    n2 = norm2_g.reshape(L, 1, D)
    fg = final_g.reshape(1, D)
    for l in range(L):
        o_lru, p_sb, p_gdn, p_ab, p_rw = _in_proj(x, mod[l], n1, prm, l)
        branches = (o_lru, _sb(p_sb), _gdn(p_gdn, p_ab, prm, l), _rw(p_rw, prm, l))
        x = _out_proj(x, mod[l], n1, branches, prm, l)
        x = _ffn(x, mod[l], n2, prm, fg, l, final=(l == L - 1))
    return x
```

```python
import functools

import jax
import jax.numpy as jnp
from jax import lax
from jax.experimental import pallas as pl
from jax.experimental.pallas import tpu as pltpu

F32 = jnp.float32
MXU_DTYPE = jnp.bfloat16

HEAD_DIM = 64
N_HEADS = 4
WIDTH = N_HEADS * HEAD_DIM
CHUNK = 64
STACK = N_HEADS * CHUNK
SUBLANES = 8
LANES = 128
EPS = 1e-6
RW_LN_EPS = 64e-5
LRU_C = 8.0
W_LORA, A_LORA, G_LORA = 32, 32, 64
FFN_SPLIT = 2

TM_DENSE = 1024
TS_REC = 1024
TQ_SB = 256
SB_GROUP = 4

VMEM_LIMIT = 56 * 1024 * 1024


def _dot(a, b):
    return lax.dot_general(a, b.astype(MXU_DTYPE), (((1,), (0,)), ((), ())),
                           preferred_element_type=F32)


def _dot_nt(a, b):
    return lax.dot_general(a, b.astype(MXU_DTYPE),
                           (((1,), (1,)), ((), ())), preferred_element_type=F32)


def _dot_tn(a, b):
    return lax.dot_general(a.astype(MXU_DTYPE), b.astype(MXU_DTYPE),
                           (((0,), (0,)), ((), ())), preferred_element_type=F32)


def _split(x, parts):
    out = []
    r = x
    for _ in range(parts - 1):
        p = r.astype(jnp.bfloat16)
        out.append(p)
        r = r - p.astype(F32)
    out.append(r.astype(jnp.bfloat16))
    return out


def _dot_x01(x, m01, parts):
    acc = None
    for p in _split(x, parts):
        t = jnp.dot(p, m01, preferred_element_type=F32)
        acc = t if acc is None else acc + t
    return acc


def _dot_01x(m01, x, parts):
    acc = None
    for p in _split(x, parts):
        t = jnp.dot(m01, p, preferred_element_type=F32)
        acc = t if acc is None else acc + t
    return acc


def _sigmoid(x):
    return jax.nn.sigmoid(x)


def _softplus(x):
    return jnp.maximum(x, 0.0) + jnp.log1p(jnp.exp(-jnp.abs(x)))


def _silu(x):
    return x * _sigmoid(x)


def _gelu_tanh(x):
    return 0.5 * x * (1.0 + jnp.tanh(0.7978845608028654 * (x + 0.044715 * (x * x * x))))


def _iota2(shape, dim):
    return lax.broadcasted_iota(jnp.int32, shape, dim)


def _shift_rows(x, k, halo):
    r = pltpu.roll(x, k, 0)
    hr = pltpu.roll(halo, k, 0)
    first = jnp.where(_iota2(halo.shape, 0) < k, hr, r[0:SUBLANES])
    return jnp.concatenate([first, r[SUBLANES:]], axis=0)


def _head_ones():
    return ((_iota2((WIDTH, WIDTH), 0) >> 6) == (_iota2((WIDTH, WIDTH), 1) >> 6)).astype(jnp.bfloat16)


def _head_sum(x, ones_bd):
    return _dot_x01(x, ones_bd, 2)


def _sbs_masks():
    t = _iota2((CHUNK, WIDTH), 0)
    j = _iota2((CHUNK, WIDTH), 1) & (CHUNK - 1)
    r = _iota2((STACK, STACK), 0)
    c = _iota2((STACK, STACK), 1)
    return dict(
        incl=t >= j,
        strict=t > j,
        m16=(t >> 4) == (j >> 4),
        m32=(t >> 5) == (j >> 5),
        eye=t == j,
        same_w=((r >> 6) == (c >> 6)).astype(MXU_DTYPE),
    )


def _bd(x, same):
    return jnp.concatenate([x.astype(MXU_DTYPE)] * N_HEADS, axis=0) * same


def _diag_blocks(full):
    lane_head = _iota2((CHUNK, WIDTH), 1) >> 6
    out = full[0:CHUNK]
    for h in range(1, N_HEADS):
        out = jnp.where(lane_head == h, full[h * CHUNK:(h + 1) * CHUNK], out)
    return out


def _expand_heads(x, lane_base):
    e = (_iota2((LANES, WIDTH), 0) == (_iota2((LANES, WIDTH), 1) >> 6) + lane_base)
    return _dot_x01(x, e.astype(jnp.bfloat16), 2)


def _inv_unit_lower(mats, m):
    n = range(len(mats))
    same = m["same_w"]
    a0 = [jnp.where(m["m16"], a, 0.0) for a in mats]
    a1 = [jnp.where(m["m32"] & jnp.logical_not(m["m16"]), a, 0.0) for a in mats]
    a2 = [jnp.where(m["m32"], 0.0, a) for a in mats]
    p = [jnp.where(m["eye"], 1.0, 0.0) - a for a in a0]
    x = [_dot(a, _bd(a, same)) for a in a0]
    for _ in range(2):
        px = [_dot(jnp.concatenate([p[i], x[i]], axis=0), _bd(x[i], same)) for i in n]
        p = [p[i] + px[i][0:CHUNK] for i in n]
        x = [px[i][CHUNK:2 * CHUNK] for i in n]
    p = [p[i] + _dot(p[i], _bd(x[i], same)) for i in n]
    for off in (a1, a2):
        t = [_dot(p[i], _bd(off[i], same)) for i in n]
        p = [p[i] - _dot(t[i], _bd(p[i], same)) for i in n]
    return p


def _modulated_norm(x, g, scale, shift):
    ms = jnp.mean(x * x, axis=-1, keepdims=True)
    return (x * lax.rsqrt(ms + EPS) * g) * (1.0 + scale) + shift


def _mod_kernel(c_ref, w_ref, b_ref, o_ref):
    c = c_ref[...]
    cond = c * _sigmoid(c)
    o_ref[...] = jnp.dot(cond, w_ref[...], precision=lax.Precision.HIGHEST,
                         preferred_element_type=F32) + b_ref[...]


def _modulation(c, w_ada, b_ada):
    L, D, D6 = w_ada.shape
    B = c.shape[0]
    nj = D6 // D
    return pl.pallas_call(
        _mod_kernel,
        grid=(L, nj),
        in_specs=[
            pl.BlockSpec((B, D), lambda l, j: (0, 0)),
            pl.BlockSpec((None, D, D), lambda l, j: (l, 0, j)),
            pl.BlockSpec((None, 1, D), lambda l, j: (l, 0, j)),
        ],
        out_specs=pl.BlockSpec((None, B, D), lambda l, j: (l, 0, j)),
        out_shape=jax.ShapeDtypeStruct((L, B, D6), F32),
        name="adaln_mod",
    )(c, w_ada, b_ada.reshape(L, 1, D6))


LRU_COLS = 2 * WIDTH
IN_COLS = (("sb", 768), ("gdn", 1024), ("ab", LANES), ("rw", 896))


def _rg_lru(p, cw, cb, wr, br, wi, bi, lam, halo_ref, h_ref, ts):
    x_in = p[:, 0:WIDTH]
    y_in = p[:, WIDTH:2 * WIDTH]
    halo = halo_ref[...]
    u = (cw[3:4] * x_in + cw[2:3] * _shift_rows(x_in, 1, halo)
         + cw[1:2] * _shift_rows(x_in, 2, halo) + cw[0:1] * _shift_rows(x_in, 3, halo) + cb)
    halo_ref[...] = x_in[ts - SUBLANES:ts]

    ub = u.astype(MXU_DTYPE)
    r = _sigmoid(jnp.dot(ub, wr, preferred_element_type=F32) + br)
    i = _sigmoid(jnp.dot(ub, wi, preferred_element_type=F32) + bi)
    log_a = (-LRU_C * r) * _softplus(-lam)
    a = jnp.exp(log_a)
    bv = jnp.sqrt(-jnp.tanh(log_a) * (a * a + 1.0)) * (i * u)

    low = _iota2((SUBLANES, WIDTH), 0)
    carry = h_ref[SUBLANES - 1:SUBLANES]
    groups = []
    for g in range(ts // SUBLANES):
        ag = a[g * SUBLANES:(g + 1) * SUBLANES]
        bg = bv[g * SUBLANES:(g + 1) * SUBLANES]
        for d in (1, 2, 4):
            bg = ag * jnp.where(low < d, 0.0, pltpu.roll(bg, d, 0)) + bg
            ag = ag * jnp.where(low < d, 1.0, pltpu.roll(ag, d, 0))
        hg = bg + ag * carry
        carry = hg[SUBLANES - 1:SUBLANES]
        groups.append(hg)
    h = jnp.concatenate(groups, axis=0)
    h_ref[...] = groups[-1]
    return h * _gelu_tanh(y_in)


def _in_kernel(x_ref, mod_ref, g_ref, w_ref, cw_ref, cb_ref, wr_ref, br_ref, wi_ref, bi_ref,
               lam_ref, o_lru, *rest, d, tm):
    o_refs, (halo_ref, h_ref) = rest[:len(IN_COLS)], rest[len(IN_COLS):]

    @pl.when(pl.program_id(1) == 0)
    def _():
        halo_ref[...] = jnp.zeros_like(halo_ref)
        h_ref[...] = jnp.zeros_like(h_ref)

    mod = mod_ref[...]
    h = _modulated_norm(x_ref[...], g_ref[...], mod[:, d:2 * d], mod[:, 0:d]).astype(MXU_DTYPE)
    p_lru = jnp.dot(h, w_ref[:, 0:LRU_COLS], preferred_element_type=F32)
    o_lru[...] = _rg_lru(p_lru, cw_ref[...], cb_ref[...], wr_ref[...], br_ref[...], wi_ref[...],
                         bi_ref[...], lam_ref[...], halo_ref, h_ref, tm)
    c0 = LRU_COLS
    for (_, n), o_ref in zip(IN_COLS, o_refs):
        o_ref[...] = jnp.dot(h, w_ref[:, c0:c0 + n], preferred_element_type=F32)
        c0 += n


def _in_proj(x, mod_l, norm_g, prm, l):
    B, S, D = x.shape
    tm = min(TM_DENSE, S)
    n_in = prm["w_in"].shape[-1]
    tok = lambda n: pl.BlockSpec((None, tm, n), lambda b, s: (b, s, 0))
    vec = lambda: pl.BlockSpec((None, 1, WIDTH), lambda b, s: (l, 0, 0))
    mat = lambda r: pl.BlockSpec((None, r, WIDTH), lambda b, s: (l, 0, 0))
    return pl.pallas_call(
        functools.partial(_in_kernel, d=D, tm=tm),
        grid=(B, S // tm),
        in_specs=[
            tok(D),
            pl.BlockSpec((None, 1, 6 * D), lambda b, s: (b, 0, 0)),
            pl.BlockSpec((None, 1, D), lambda b, s: (l, 0, 0)),
            pl.BlockSpec((None, D, n_in), lambda b, s: (l, 0, 0), pipeline_mode=pl.Buffered(1)),
            mat(4), vec(), mat(WIDTH), vec(), mat(WIDTH), vec(), vec(),
        ],
        out_specs=[tok(WIDTH)] + [tok(n) for _, n in IN_COLS],
        out_shape=[jax.ShapeDtypeStruct((B, S, WIDTH), F32)]
        + [jax.ShapeDtypeStruct((B, S, n), F32) for _, n in IN_COLS],
        scratch_shapes=[pltpu.VMEM((SUBLANES, WIDTH), F32), pltpu.VMEM((SUBLANES, WIDTH), F32)],
        compiler_params=pltpu.CompilerParams(
            dimension_semantics=("parallel", "arbitrary"), vmem_limit_bytes=VMEM_LIMIT),
        name="in_proj_lru",
    )(x, mod_l, norm_g, prm["w_in"], prm["lru_conv_w"], prm["lru_conv_b"], prm["lru_wr"],
      prm["lru_b_r"], prm["lru_wi"], prm["lru_b_i"], prm["lru_lambda"])


def _sb_kernel(q_ref, k_ref, v_ref, o_ref, acc_ref, *, tq):
    i = pl.program_id(1)
    q = q_ref[...] * (HEAD_DIM ** -0.5)
    head = _iota2((1, WIDTH), 1) >> 6
    hm = [head == h for h in range(N_HEADS)]
    qh = [jnp.where(hm[h], q, 0.0).astype(MXU_DTYPE) for h in range(N_HEADS)]
    row = _iota2((tq, tq), 0)
    col = _iota2((tq, tq), 1)
    rev_incl = (row >= col).astype(jnp.bfloat16)
    past = col < row
    acc_ref[...] = jnp.zeros_like(acc_ref)

    def tiles(first, count, carries, diag):
        heads = range(N_HEADS)
        group = range(count)
        units = [(g, h) for g in group for h in heads]
        starts = [pl.multiple_of((first - g) * tq, tq) for g in group]
        kts = [k_ref[pl.ds(s, tq), :].astype(MXU_DTYPE) for s in starts]
        vts = [v_ref[pl.ds(s, tq), :].astype(MXU_DTYPE) for s in starts]
        zs = {(g, h): lax.dot_general(qh[h], kts[g], (((1,), (1,)), ((), ())),
                                      preferred_element_type=F32) for g, h in units}
        sps = {u: jnp.maximum(zs[u], 0.0) + jnp.log(1.0 + jnp.exp(-jnp.abs(zs[u]))) for u in units}
        if diag:
            sps.update({(0, h): jnp.where(past, sps[(0, h)], 0.0) for h in heads})
        part = {u: _dot_x01(sps[u], rev_incl, 2) for u in units}
        css = {}
        new = []
        for h in heads:
            c = carries[h]
            for g in group:
                css[(g, h)] = part[(g, h)] + c
                c = css[(g, h)][:, 0:1]
            new.append(c)
        ws = {u: jnp.exp(zs[u] - css[u]) for u in units}
        if diag:
            ws.update({(0, h): jnp.where(past, ws[(0, h)], 0.0) for h in heads})
        total = None
        for g in group:
            pv = None
            for h in heads:
                o = _dot(ws[(g, h)], vts[g])
                pv = o if pv is None else jnp.where(hm[h], o, pv)
            total = pv if total is None else total + pv
        acc_ref[...] += total
        return tuple(new)

    zero = (jnp.zeros((tq, 1), F32),) * N_HEADS
    extra = i % SB_GROUP
    carries = lax.switch(extra, [functools.partial(tiles, i, n + 1, zero, True)
                                 for n in range(SB_GROUP)])
    lax.fori_loop(0, i // SB_GROUP,
                  lambda it, c: tiles(i - extra - 1 - it * SB_GROUP, SB_GROUP, c, False), carries)
    o_ref[...] = acc_ref[...]


def _sb(p_sb):
    B, S, _ = p_sb.shape
    tq = min(TQ_SB, S)
    return pl.pallas_call(
        functools.partial(_sb_kernel, tq=tq),
        grid=(B, S // tq),
        in_specs=[
            pl.BlockSpec((None, tq, WIDTH), lambda b, i: (b, i, 0)),
            pl.BlockSpec((None, S, WIDTH), lambda b, i: (b, 0, 1)),
            pl.BlockSpec((None, S, WIDTH), lambda b, i: (b, 0, 2)),
        ],
        out_specs=pl.BlockSpec((None, tq, WIDTH), lambda b, i: (b, i, 0)),
        out_shape=jax.ShapeDtypeStruct((B, S, WIDTH), F32),
        scratch_shapes=[pltpu.VMEM((tq, WIDTH), F32)],
        compiler_params=pltpu.CompilerParams(
            dimension_semantics=("parallel", "arbitrary"), vmem_limit_bytes=VMEM_LIMIT),
        name="stick_breaking",
    )(p_sb, p_sb, p_sb)


def _gdn_kernel(p_ref, ab_ref, cw_ref, alog_ref, dtb_ref, ng_ref, o_ref,
                halo_ref, st_ref, *, ts):
    @pl.when(pl.program_id(1) == 0)
    def _():
        halo_ref[...] = jnp.zeros_like(halo_ref)
        st_ref[...] = jnp.zeros_like(st_ref)

    qkv = p_ref[:, 0:3 * WIDTH]
    halo = halo_ref[...]
    cw = cw_ref[...]
    c = (cw[3:4] * qkv + cw[2:3] * _shift_rows(qkv, 1, halo)
         + cw[1:2] * _shift_rows(qkv, 2, halo) + cw[0:1] * _shift_rows(qkv, 3, halo))
    halo_ref[...] = qkv[ts - SUBLANES:ts]
    c = _silu(c)
    ones_bd = _head_ones()
    q = c[:, 0:WIDTH]
    k = c[:, WIDTH:2 * WIDTH]
    qn = q * lax.rsqrt(_head_sum(q * q, ones_bd) + EPS) * (HEAD_DIM ** -0.5)
    kn = k * lax.rsqrt(_head_sum(k * k, ones_bd) + EPS)
    v = c[:, 2 * WIDTH:3 * WIDTH]
    ab = ab_ref[...]
    g = -jnp.exp(alog_ref[...]) * _softplus(ab + dtb_ref[...])
    beta = _sigmoid(ab)

    gexp = _expand_heads(g, 0)
    bx = _expand_heads(beta, N_HEADS)

    m = _sbs_masks()
    same = m["same_w"]
    cum_incl = (_iota2((CHUNK, CHUNK), 0) >= _iota2((CHUNK, CHUNK), 1)).astype(jnp.bfloat16)
    ones_cc = jnp.ones((CHUNK, CHUNK), jnp.bfloat16)
    chunks = range(ts // CHUNK)
    rows = [slice(ci * CHUNK, (ci + 1) * CHUNK) for ci in chunks]
    lo, hi = slice(0, CHUNK), slice(CHUNK, 2 * CHUNK)
    left, right = slice(0, STACK), slice(STACK, 2 * STACK)

    gx = [_dot_01x(cum_incl, gexp[r], 3) for r in rows]
    g_row = [_dot_01x(ones_cc, jnp.where(m["eye"], x, 0.0), 3) for x in gx]
    decay = [jnp.exp(jnp.where(m["incl"], gx[i] - g_row[i], -1e30)) for i in chunks]
    k_beta = [kn[rows[i]] * bx[rows[i]] for i in chunks]
    kq = [_dot_nt(jnp.concatenate([k_beta[i], qn[rows[i]]], axis=0), _bd(kn[rows[i]], same))
          for i in chunks]
    a = [jnp.where(m["strict"], kq[i][lo] * decay[i], 0.0) for i in chunks]
    qk = [jnp.where(m["incl"], kq[i][hi] * decay[i], 0.0) for i in chunks]
    t = _inv_unit_lower(a, m)
    eg = [jnp.exp(x) for x in gx]
    sol = [_dot(t[i], jnp.concatenate([_bd(v[rows[i]] * bx[rows[i]], same),
                                       _bd(k_beta[i] * eg[i], same)], axis=1))
           for i in chunks]
    gl = [x[CHUNK - 1:CHUNK] for x in gx]
    k_dec = [kn[rows[i]] * jnp.exp(gl[i] - gx[i]) for i in chunks]
    nm = [_dot_tn(sol[i], k_dec[i]) for i in chunks]
    ut_kd = [_diag_blocks(x[left]) for x in nm]
    wt_kd = [_bd(_diag_blocks(x[right]), same) for x in nm]
    wq = [jnp.concatenate([sol[i][:, right], qn[rows[i]] * eg[i]], axis=0) for i in chunks]

    state = st_ref[...]
    ws, outs = [], []

    def finish(i):
        u = sol[i][:, left] - ws[i][lo]
        outs.append(ws[i][hi] + _dot(qk[i], _bd(u, same)))

    for i in chunks:
        prev = state
        state = state * jnp.exp(gl[i]) - _dot(state, wt_kd[i]) + ut_kd[i]
        ws.append(_dot_nt(wq[i], _bd(prev, same)))
        if i > 0:
            finish(i - 1)
    finish(chunks[-1])
    st_ref[...] = state
    o = jnp.concatenate(outs, axis=0)
    o = o * lax.rsqrt(_head_sum(o * o, ones_bd) * (1.0 / HEAD_DIM) + EPS) * ng_ref[...]
    o_ref[...] = o * _silu(p_ref[:, 3 * WIDTH:4 * WIDTH])


def _gdn(p_gdn, p_ab, prm, l):
    B, S, _ = p_gdn.shape
    ts = min(TS_REC, S)
    return pl.pallas_call(
        functools.partial(_gdn_kernel, ts=ts),
        grid=(B, S // ts),
        in_specs=[
            pl.BlockSpec((None, ts, 4 * WIDTH), lambda b, s: (b, s, 0)),
            pl.BlockSpec((None, ts, LANES), lambda b, s: (b, s, 0)),
            pl.BlockSpec((None, 4, 3 * WIDTH), lambda b, s: (l, 0, 0)),
            pl.BlockSpec((None, 1, LANES), lambda b, s: (l, 0, 0)),
            pl.BlockSpec((None, 1, LANES), lambda b, s: (l, 0, 0)),
            pl.BlockSpec((None, 1, WIDTH), lambda b, s: (l, 0, 0)),
        ],
        out_specs=pl.BlockSpec((None, ts, WIDTH), lambda b, s: (b, s, 0)),
        out_shape=jax.ShapeDtypeStruct((B, S, WIDTH), F32),
        scratch_shapes=[pltpu.VMEM((SUBLANES, 3 * WIDTH), F32), pltpu.VMEM((CHUNK, WIDTH), F32)],
        compiler_params=pltpu.CompilerParams(
            dimension_semantics=("parallel", "arbitrary"), vmem_limit_bytes=VMEM_LIMIT),
        name="gated_deltanet",
    )(p_gdn, p_ab, prm["gdn_conv_w"], prm["gdn_a_log"], prm["gdn_dt_bias"], prm["gdn_norm_g"])


def _rw_kernel(p_ref, mu_ref, w0_ref, wup_ref, a0_ref, aup_ref, gup_ref, kk_ref, ka_ref, rk_ref,
               lng_ref, lnb_ref, o_ref,
               halo_ref, st_ref, *, ts):
    @pl.when(pl.program_id(1) == 0)
    def _():
        halo_ref[...] = jnp.zeros_like(halo_ref)
        st_ref[...] = jnp.zeros_like(st_ref)

    p = p_ref[...]
    prev = _shift_rows(p, 1, halo_ref[...])
    halo_ref[...] = p[ts - SUBLANES:ts]
    pf = p + (prev - p) * mu_ref[...]
    r = pf[:, 0:WIDTH]
    k = pf[:, WIDTH:2 * WIDTH]
    v = pf[:, 2 * WIDTH:3 * WIDTH]
    x = pf[:, 3 * WIDTH:3 * WIDTH + LANES]
    lw = _dot(jnp.tanh(x), wup_ref[...])
    la = _dot(x, aup_ref[...])
    gate = _dot(_sigmoid(x), gup_ref[...])
    w_log = -_softplus(-(w0_ref[...] + lw)) - 0.5
    a = _sigmoid(a0_ref[...] + la)
    ones_bd = _head_ones()
    kkv = k * kk_ref[...]
    kk = kkv * lax.rsqrt(_head_sum(kkv * kkv, ones_bd) + EPS)
    k = k * (1.0 + (a - 1.0) * ka_ref[...])
    b = kk * a
    ld = -jnp.exp(w_log)

    m = _sbs_masks()
    same = m["same_w"]
    cum_incl = (_iota2((CHUNK, CHUNK), 0) >= _iota2((CHUNK, CHUNK), 1)).astype(jnp.bfloat16)
    chunks = range(ts // CHUNK)
    rows = [slice(ci * CHUNK, (ci + 1) * CHUNK) for ci in chunks]
    bd = lambda t: _bd(t, same)
    lo, hi = slice(0, CHUNK), slice(CHUNK, 2 * CHUNK)
    left, right = slice(0, STACK), slice(STACK, 2 * STACK)

    lg = [_dot_01x(cum_incl, ld[r_], 2) for r_ in rows]
    ll = [x[CHUNK - 1:CHUNK] for x in lg]
    e_neg = [jnp.exp(-x) for x in lg]
    e_rest = [jnp.exp(ll[i] - lg[i]) for i in chunks]
    kt = [kk[rows[i]] * jnp.exp(lg[i] - ld[rows[i]]) for i in chunks]
    rt = [r[rows[i]] * jnp.exp(lg[i]) for i in chunks]
    kh = [k[rows[i]] * e_neg[i] for i in chunks]
    bh = [b[rows[i]] * e_neg[i] for i in chunks]
    kbar = [k[rows[i]] * e_rest[i] for i in chunks]
    bbar = [b[rows[i]] * e_rest[i] for i in chunks]
    big = [_dot_nt(jnp.concatenate([kt[i], rt[i]], axis=0),
                   jnp.concatenate([bd(bh[i]), bd(kh[i])], axis=0)) for i in chunks]
    a_kb = [jnp.where(m["strict"], x[lo, left], 0.0) for x in big]
    a_kk = [jnp.where(m["strict"], x[lo, right], 0.0) for x in big]
    a_rb = [jnp.where(m["incl"], x[hi, left], 0.0) for x in big]
    a_rk = [jnp.where(m["incl"], x[hi, right], 0.0) for x in big]
    t = _inv_unit_lower(a_kb, m)
    akv = [_dot(jnp.concatenate([a_kk[i], a_rk[i]], axis=0), bd(v[rows[i]])) for i in chunks]
    tk = [_dot(t[i], jnp.concatenate([bd(akv[i][lo]), bd(kt[i])], axis=1)) for i in chunks]
    tb = [_dot_tn(tk[i], bbar[i]) for i in chunks]
    ub = [_diag_blocks(x[left]) for x in tb]
    wb = [bd(_diag_blocks(x[right])) for x in tb]
    vk = [_diag_blocks(_dot_tn(v[rows[i]], kbar[i])) for i in chunks]
    wr = [jnp.concatenate([tk[i][:, right], rt[i]], axis=0) for i in chunks]

    state = st_ref[...]
    ws, outs = [], []

    def finish(i):
        u = ws[i][lo] + tk[i][:, left]
        outs.append(ws[i][hi] + akv[i][hi] - _dot(a_rb[i], bd(u)))

    for i in chunks:
        prev = state
        state = state * jnp.exp(ll[i]) - _dot(state, wb[i]) + (vk[i] - ub[i])
        ws.append(_dot_nt(wr[i], bd(prev)))
        if i > 0:
            finish(i - 1)
    finish(chunks[-1])
    st_ref[...] = state
    y = jnp.concatenate(outs, axis=0)
    mean =_head_sum(y, ones_bd) * (1.0 / HEAD_DIM)
    yc = y - mean
    var = _head_sum(yc * yc, ones_bd) * (1.0 / HEAD_DIM)
    yn = yc * lax.rsqrt(var + RW_LN_EPS) * lng_ref[...] + lnb_ref[...]
    bonus = _head_sum(r * k * rk_ref[...], ones_bd) * v
    o_ref[...] = (yn + bonus) * gate


def _rw(p_rw, prm, l):
    B, S, n_rw = p_rw.shape
    ts = min(TS_REC, S)
    vec = lambda n=WIDTH: pl.BlockSpec((None, 1, n), lambda b, s: (l, 0, 0))
    lora = lambda: pl.BlockSpec((None, LANES, WIDTH), lambda b, s: (l, 0, 0))
    return pl.pallas_call(
        functools.partial(_rw_kernel, ts=ts),
        grid=(B, S // ts),
        in_specs=[
            pl.BlockSpec((None, ts, n_rw), lambda b, s: (b, s, 0)),
            vec(n_rw), vec(), lora(), vec(), lora(), lora(), vec(), vec(), vec(), vec(), vec(),
        ],
        out_specs=pl.BlockSpec((None, ts, WIDTH), lambda b, s: (b, s, 0)),
        out_shape=jax.ShapeDtypeStruct((B, S, WIDTH), F32),
        scratch_shapes=[pltpu.VMEM((SUBLANES, n_rw), F32), pltpu.VMEM((CHUNK, WIDTH), F32)],
        compiler_params=pltpu.CompilerParams(
            dimension_semantics=("parallel", "arbitrary"), vmem_limit_bytes=VMEM_LIMIT),
        name="rwkv7",
    )(p_rw, prm["rw_mu"], prm["rw_w0"], prm["rw_wup"], prm["rw_a0"], prm["rw_aup"], prm["rw_gup"],
      prm["rw_k_k"], prm["rw_k_a"], prm["rw_r_k"], prm["rw_ln_g"], prm["rw_ln_b"])


def _out_kernel(x_ref, mod_ref, g_ref, o0_ref, o1_ref, o2_ref, o3_ref,
                wg_ref, bg_ref, wb_ref, wo_ref, xo_ref, *, d):
    x = x_ref[...]
    mod = mod_ref[...]
    h = _modulated_norm(x, g_ref[...], mod[:, d:2 * d], mod[:, 0:d]).astype(MXU_DTYPE)
    mixed = None
    for n, o_ref in enumerate((o0_ref, o1_ref, o2_ref, o3_ref)):
        gate = _sigmoid(jnp.dot(h, wg_ref[n], preferred_element_type=F32) + bg_ref[n])
        y = jnp.dot(o_ref[...].astype(MXU_DTYPE), wb_ref[n], preferred_element_type=F32)
        mixed = gate * y if mixed is None else mixed + gate * y
    out = jnp.dot(mixed.astype(MXU_DTYPE), wo_ref[...], preferred_element_type=F32)
    xo_ref[...] = x + mod[:, 2 * d:3 * d] * out


def _out_proj(x, mod_l, norm_g, branches, prm, l):
    B, S, D = x.shape
    tm = min(TM_DENSE, S)
    tok = lambda n: pl.BlockSpec((None, tm, n), lambda b, s: (b, s, 0))
    const = lambda shape: pl.BlockSpec((None,) + shape, lambda b, s: (l,) + (0,) * len(shape),
                                       pipeline_mode=pl.Buffered(1))
    return pl.pallas_call(
        functools.partial(_out_kernel, d=D),
        grid=(B, S // tm),
        in_specs=[
            tok(D),
            pl.BlockSpec((None, 1, 6 * D), lambda b, s: (b, 0, 0)),
            const((1, D)),
            tok(WIDTH), tok(WIDTH), tok(WIDTH), tok(WIDTH),
            const((N_HEADS, D, D)), const((N_HEADS, 1, D)), const((N_HEADS, WIDTH, D)), const((D, D)),
        ],
        out_specs=tok(D),
        out_shape=jax.ShapeDtypeStruct((B, S, D), F32),
        compiler_params=pltpu.CompilerParams(
            dimension_semantics=("parallel", "parallel"), vmem_limit_bytes=VMEM_LIMIT),
        name="out_proj",
    )(x, mod_l, norm_g, *branches, prm["w_gate"], prm["b_gate"], prm["w_branch"], prm["w_out"])


def _ffn_kernel(x_ref, mod_ref, g_ref, wg_ref, wu_ref, cw_ref, wd_ref, fg_ref, xo_ref, halo_ref,
                *, d, tm, fc, final):
    @pl.when(pl.program_id(1) == 0)
    def _():
        halo_ref[...] = jnp.zeros_like(halo_ref)

    x = x_ref[...]
    mod = mod_ref[...]
    h = _modulated_norm(x, g_ref[...], mod[:, 4 * d:5 * d], mod[:, 3 * d:4 * d]).astype(MXU_DTYPE)
    acc = None
    for j in range(FFN_SPLIT):
        cols = slice(j * fc, (j + 1) * fc)
        a_pre = jnp.dot(h, wg_ref[:, cols], preferred_element_type=F32)
        up = jnp.dot(h, wu_ref[:, cols], preferred_element_type=F32)
        halo = halo_ref[:, cols]
        cw = cw_ref[:, cols]
        a = (cw[2:3] * a_pre + cw[1:2] * _shift_rows(a_pre, 1, halo)
             + cw[0:1] * _shift_rows(a_pre, 2, halo))
        halo_ref[:, cols] = a_pre[tm - SUBLANES:tm]
        act = (_gelu_tanh(a) * up).astype(MXU_DTYPE)
        part = jnp.dot(act, wd_ref[cols, :], preferred_element_type=F32)
        acc = part if acc is None else acc + part
    y = x + mod[:, 5 * d:6 * d] * acc
    if final:
        y = y * lax.rsqrt(jnp.mean(y * y, axis=-1, keepdims=True) + EPS) * fg_ref[...]
    xo_ref[...] = y


def _ffn(x, mod_l, norm_g, prm, final_g, l, final):
    B, S, D = x.shape
    F = prm["ffn_w_gate"].shape[-1]
    tm = min(TM_DENSE, S)
    fc = F // FFN_SPLIT
    tok = lambda n: pl.BlockSpec((None, tm, n), lambda b, s: (b, s, 0))
    const = lambda shape: pl.BlockSpec((None,) + shape, lambda b, s: (l,) + (0,) * len(shape),
                                       pipeline_mode=pl.Buffered(1))
    return pl.pallas_call(
        functools.partial(_ffn_kernel, d=D, tm=tm, fc=fc, final=final),
        grid=(B, S // tm),
        in_specs=[
            tok(D),
            pl.BlockSpec((None, 1, 6 * D), lambda b, s: (b, 0, 0)),
            const((1, D)),
            const((D, F)), const((D, F)), const((3, F)), const((F, D)),
            pl.BlockSpec((1, D), lambda b, s: (0, 0)),
        ],
        out_specs=tok(D),
        out_shape=jax.ShapeDtypeStruct((B, S, D), F32),
        scratch_shapes=[pltpu.VMEM((SUBLANES, F), F32)],
        compiler_params=pltpu.CompilerParams(
            dimension_semantics=("parallel", "arbitrary"), vmem_limit_bytes=VMEM_LIMIT),
        name="conv_ffn",
    )(x, mod_l, norm_g, prm["ffn_w_gate"], prm["ffn_w_up"], prm["ffn_conv_w"], prm["ffn_w_down"],
      final_g)


def _prepare(w_in, lru_conv_w, lru_conv_b, lru_w_r, lru_b_r, lru_w_i, lru_b_i, lru_lambda,
             gdn_conv_w, gdn_a_log, gdn_dt_bias, gdn_norm_g,
             rw_mu, rw_w0, rw_w_up, rw_a0, rw_a_up, rw_g_up, rw_k_k, rw_k_a, rw_r_k, rw_ln_g, rw_ln_b,
             w_branch, w_gate, b_gate, w_out, ffn_w_gate, ffn_w_up, ffn_conv_w, ffn_w_down):
    L, D, _ = w_in.shape
    bf = lambda t: t.astype(MXU_DTYPE)
    row = lambda t: t.reshape(L, 1, -1)

    n_ab = 2 * N_HEADS
    c_ab = 512 + 768 + 1024
    w_in_p = jnp.concatenate(
        [w_in[:, :, :c_ab], w_in[:, :, c_ab:c_ab + n_ab],
         jnp.zeros((L, D, LANES - n_ab), w_in.dtype), w_in[:, :, c_ab + n_ab:]], axis=-1)

    def block_diag(w):
        eye = jnp.eye(N_HEADS, dtype=w.dtype)
        return jnp.einsum("lnef,nm->lnemf", w, eye).reshape(L, WIDTH, WIDTH)

    def lane_pad(t, lo):
        return jnp.pad(t, ((0, 0), (lo, LANES - lo - t.shape[1]))).reshape(L, 1, LANES)

    def lora_pad(w, lo):
        return jnp.pad(w, ((0, 0), (lo, LANES - lo - w.shape[1]), (0, 0)))

    return dict(
        w_in=bf(w_in_p),
        lru_conv_w=lru_conv_w, lru_conv_b=row(lru_conv_b),
        lru_wr=bf(block_diag(lru_w_r)), lru_b_r=row(lru_b_r),
        lru_wi=bf(block_diag(lru_w_i)), lru_b_i=row(lru_b_i), lru_lambda=row(lru_lambda),
        gdn_conv_w=gdn_conv_w, gdn_a_log=lane_pad(gdn_a_log, 0), gdn_dt_bias=lane_pad(gdn_dt_bias, 0),
        gdn_norm_g=row(jnp.tile(gdn_norm_g, (1, N_HEADS))),
        rw_mu=row(rw_mu), rw_w0=row(rw_w0), rw_a0=row(rw_a0),
        rw_wup=bf(lora_pad(rw_w_up, 0)), rw_aup=bf(lora_pad(rw_a_up, W_LORA)),
        rw_gup=bf(lora_pad(rw_g_up, W_LORA + A_LORA)),
        rw_k_k=row(rw_k_k), rw_k_a=row(rw_k_a), rw_r_k=row(rw_r_k),
        rw_ln_g=row(rw_ln_g), rw_ln_b=row(rw_ln_b),
        w_gate=bf(w_gate), b_gate=b_gate.reshape(L, N_HEADS, 1, D), w_branch=bf(w_branch), w_out=bf(w_out),
        ffn_w_gate=bf(ffn_w_gate), ffn_w_up=bf(ffn_w_up), ffn_conv_w=ffn_conv_w, ffn_w_down=bf(ffn_w_down),
    )


def kernel(x, c, norm1_g, norm2_g, final_g, w_ada, b_ada, w_in, lru_conv_w, lru_conv_b, lru_w_r, lru_b_r, lru_w_i, lru_b_i, lru_lambda, gdn_conv_w, gdn_a_log, gdn_dt_bias, gdn_norm_g, rw_mu, rw_w0, rw_w_up, rw_a0, rw_a_up, rw_g_up, rw_k_k, rw_k_a, rw_r_k, rw_ln_g, rw_ln_b, w_branch, w_gate, b_gate, w_out, ffn_w_gate, ffn_w_up, ffn_conv_w, ffn_w_down):
    B, S, D = x.shape
    L = w_in.shape[0]
    prm = _prepare(w_in, lru_conv_w, lru_conv_b, lru_w_r, lru_b_r, lru_w_i, lru_b_i, lru_lambda,
                   gdn_conv_w, gdn_a_log, gdn_dt_bias, gdn_norm_g,
                   rw_mu, rw_w0, rw_w_up, rw_a0, rw_a_up, rw_g_up, rw_k_k, rw_k_a, rw_r_k,
                   rw_ln_g, rw_ln_b, w_branch, w_gate, b_gate, w_out,
                   ffn_w_gate, ffn_w_up, ffn_conv_w, ffn_w_down)
    mod = _modulation(c, w_ada, b_ada).reshape(L, B, 1, 6 * D)
    n1 = norm1_g.reshape(L, 1, D)
    n2 = norm2_g.reshape(L, 1, D)
    fg = final_g.reshape(1, D)
    for l in range(L):
        o_lru, p_sb, p_gdn, p_ab, p_rw = _in_proj(x, mod[l], n1, prm, l)
        branches = (o_lru, _sb(p_sb), _gdn(p_gdn, p_ab, prm, l), _rw(p_rw, prm, l))
        x = _out_proj(x, mod[l], n1, branches, prm, l)
        x = _ffn(x, mod[l], n2, prm, fg, l, final=(l == L - 1))
    return x
```

```python
import functools

import jax
import jax.numpy as jnp
from jax import lax
from jax.experimental import pallas as pl
from jax.experimental.pallas import tpu as pltpu

F32 = jnp.float32
MXU_DTYPE = jnp.bfloat16

HEAD_DIM = 64
N_HEADS = 4
WIDTH = N_HEADS * HEAD_DIM
CHUNK = 64
STACK = N_HEADS * CHUNK
SUBLANES = 8
LANES = 128
EPS = 1e-6
LOG2_E = 1.4426950408889634
RW_LN_EPS = 64e-5
LRU_C = 8.0
W_LORA, A_LORA, G_LORA = 32, 32, 64
FFN_SPLIT = 2

TM_DENSE = 1024
TS_REC = 1024
TQ_SB = 256
SB_GROUP = 4

VMEM_LIMIT = 56 * 1024 * 1024


def _tile(seq, preferred):
    tile = min(preferred, seq)
    assert seq % tile == 0 and tile % CHUNK == 0, (seq, tile)
    return tile


def _dot(a, b):
    return lax.dot_general(a, b.astype(MXU_DTYPE), (((1,), (0,)), ((), ())),
                           preferred_element_type=F32)


def _dot_nt(a, b):
    return lax.dot_general(a.astype(MXU_DTYPE), b.astype(MXU_DTYPE),
                           (((1,), (1,)), ((), ())), preferred_element_type=F32)


def _dot_tn(a, b):
    return lax.dot_general(a.astype(MXU_DTYPE), b.astype(MXU_DTYPE),
                           (((0,), (0,)), ((), ())), preferred_element_type=F32)


def _split(x, parts):
    out = []
    r = x
    for _ in range(parts - 1):
        p = r.astype(jnp.bfloat16)
        out.append(p)
        r = r - p.astype(F32)
    out.append(r.astype(jnp.bfloat16))
    return out


def _dot_x01(x, m01, parts):
    acc = None
    for p in _split(x, parts):
        t = jnp.dot(p, m01, preferred_element_type=F32)
        acc = t if acc is None else acc + t
    return acc


def _dot_01x(m01, x, parts):
    acc = None
    for p in _split(x, parts):
        t = jnp.dot(m01, p, preferred_element_type=F32)
        acc = t if acc is None else acc + t
    return acc


def _sigmoid(x):
    return jax.nn.sigmoid(x)


def _softplus(x):
    return jnp.maximum(x, 0.0) + jnp.log1p(jnp.exp(-jnp.abs(x)))


def _silu(x):
    return x * _sigmoid(x)


def _gelu_tanh(x):
    return 0.5 * x * (1.0 + jnp.tanh(0.7978845608028654 * (x + 0.044715 * (x * x * x))))


def _iota2(shape, dim):
    return lax.broadcasted_iota(jnp.int32, shape, dim)


def _shift_rows(x, k, halo):
    r = pltpu.roll(x, k, 0)
    hr = pltpu.roll(halo, k, 0)
    first = jnp.where(_iota2(halo.shape, 0) < k, hr, r[0:SUBLANES])
    return jnp.concatenate([first, r[SUBLANES:]], axis=0)


def _head_ones():
    return ((_iota2((WIDTH, WIDTH), 0) >> 6) == (_iota2((WIDTH, WIDTH), 1) >> 6)).astype(jnp.bfloat16)


def _head_sum(x, ones_bd):
    return _dot_x01(x, ones_bd, 2)


def _sbs_masks():
    t = _iota2((CHUNK, WIDTH), 0)
    j = _iota2((CHUNK, WIDTH), 1) & (CHUNK - 1)
    r = _iota2((STACK, STACK), 0)
    c = _iota2((STACK, STACK), 1)
    return dict(
        incl=t >= j,
        strict=t > j,
        m16=(t >> 4) == (j >> 4),
        m32=(t >> 5) == (j >> 5),
        eye=t == j,
        same_w=((r >> 6) == (c >> 6)).astype(MXU_DTYPE),
    )


def _bd(x, same):
    return jnp.concatenate([x.astype(MXU_DTYPE)] * N_HEADS, axis=0) * same


def _diag_blocks(full):
    lane_head = _iota2((CHUNK, WIDTH), 1) >> 6
    out = full[0:CHUNK]
    for h in range(1, N_HEADS):
        out = jnp.where(lane_head == h, full[h * CHUNK:(h + 1) * CHUNK], out)
    return out


def _expand_heads(x, lane_base):
    e = (_iota2((LANES, WIDTH), 0) == (_iota2((LANES, WIDTH), 1) >> 6) + lane_base)
    return _dot_x01(x, e.astype(jnp.bfloat16), 2)


def _inv_unit_lower(mats, m):
    n = range(len(mats))
    same = m["same_w"]
    a0 = [jnp.where(m["m16"], a, 0.0) for a in mats]
    a1 = [jnp.where(m["m32"] & jnp.logical_not(m["m16"]), a, 0.0) for a in mats]
    a2 = [jnp.where(m["m32"], 0.0, a) for a in mats]
    p = [jnp.where(m["eye"], 1.0, 0.0) - a for a in a0]
    x = [_dot(a, _bd(a, same)) for a in a0]
    for _ in range(2):
        px = [_dot(jnp.concatenate([p[i], x[i]], axis=0), _bd(x[i], same)) for i in n]
        p = [p[i] + px[i][0:CHUNK] for i in n]
        x = [px[i][CHUNK:2 * CHUNK] for i in n]
    p = [p[i] + _dot(p[i], _bd(x[i], same)) for i in n]
    for off in (a1, a2):
        t = [_dot(p[i], _bd(off[i], same)) for i in n]
        p = [p[i] - _dot(t[i], _bd(p[i], same)) for i in n]
    return p


def _modulated_norm(x, g, scale, shift):
    ms = jnp.mean(x * x, axis=-1, keepdims=True)
    return (x * lax.rsqrt(ms + EPS) * g) * (1.0 + scale) + shift


def _mod_kernel(c_ref, w_ref, b_ref, o_ref):
    c = c_ref[...]
    cond = c * _sigmoid(c)
    o_ref[...] = jnp.dot(cond, w_ref[...], precision=lax.Precision.HIGHEST,
                         preferred_element_type=F32) + b_ref[...]


def _modulation(c, w_ada, b_ada):
    L, D, D6 = w_ada.shape
    B = c.shape[0]
    nj = D6 // D
    return pl.pallas_call(
        _mod_kernel,
        grid=(L, nj),
        in_specs=[
            pl.BlockSpec((B, D), lambda l, j: (0, 0)),
            pl.BlockSpec((None, D, D), lambda l, j: (l, 0, j)),
            pl.BlockSpec((None, 1, D), lambda l, j: (l, 0, j)),
        ],
        out_specs=pl.BlockSpec((None, B, D), lambda l, j: (l, 0, j)),
        out_shape=jax.ShapeDtypeStruct((L, B, D6), F32),
        name="adaln_mod",
    )(c, w_ada, b_ada.reshape(L, 1, D6))


LRU_COLS = 2 * WIDTH
IN_COLS = (("sb", 768), ("gdn", 1024), ("ab", LANES), ("rw", 896))


def _rg_lru(p, cw, cb, wr, br, wi, bi, lam, halo_ref, h_ref, ts):
    x_in = p[:, 0:WIDTH]
    y_in = p[:, WIDTH:2 * WIDTH]
    halo = halo_ref[...]
    u = (cw[3:4] * x_in + cw[2:3] * _shift_rows(x_in, 1, halo)
         + cw[1:2] * _shift_rows(x_in, 2, halo) + cw[0:1] * _shift_rows(x_in, 3, halo) + cb)
    halo_ref[...] = x_in[ts - SUBLANES:ts]

    ub = u.astype(MXU_DTYPE)
    r = _sigmoid(jnp.dot(ub, wr, preferred_element_type=F32) + br)
    i = _sigmoid(jnp.dot(ub, wi, preferred_element_type=F32) + bi)
    log_a = (-LRU_C * r) * _softplus(-lam)
    a = jnp.exp(log_a)
    bv = jnp.sqrt(-jnp.tanh(log_a) * (a * a + 1.0)) * (i * u)

    low = _iota2((SUBLANES, WIDTH), 0)
    carry = h_ref[SUBLANES - 1:SUBLANES]
    groups = []
    for g in range(ts // SUBLANES):
        ag = a[g * SUBLANES:(g + 1) * SUBLANES]
        bg = bv[g * SUBLANES:(g + 1) * SUBLANES]
        for d in (1, 2, 4):
            bg = ag * jnp.where(low < d, 0.0, pltpu.roll(bg, d, 0)) + bg
            ag = ag * jnp.where(low < d, 1.0, pltpu.roll(ag, d, 0))
        hg = bg + ag * carry
        carry = hg[SUBLANES - 1:SUBLANES]
        groups.append(hg)
    h = jnp.concatenate(groups, axis=0)
    h_ref[...] = groups[-1]
    return h * _gelu_tanh(y_in)


def _in_kernel(x_ref, mod_ref, g_ref, w_ref, cw_ref, cb_ref, wr_ref, br_ref, wi_ref, bi_ref,
               lam_ref, o_lru, *rest, d, tm):
    o_refs, (halo_ref, h_ref) = rest[:len(IN_COLS)], rest[len(IN_COLS):]

    @pl.when(pl.program_id(1) == 0)
    def _():
        halo_ref[...] = jnp.zeros_like(halo_ref)
        h_ref[...] = jnp.zeros_like(h_ref)

    mod = mod_ref[...]
    h = _modulated_norm(x_ref[...], g_ref[...], mod[:, d:2 * d], mod[:, 0:d]).astype(MXU_DTYPE)
    p_lru = jnp.dot(h, w_ref[:, 0:LRU_COLS], preferred_element_type=F32)
    o_lru[...] = _rg_lru(p_lru, cw_ref[...], cb_ref[...], wr_ref[...], br_ref[...], wi_ref[...],
                         bi_ref[...], lam_ref[...], halo_ref, h_ref, tm)
    c0 = LRU_COLS
    for (_, n), o_ref in zip(IN_COLS, o_refs):
        o_ref[...] = jnp.dot(h, w_ref[:, c0:c0 + n], preferred_element_type=F32)
        c0 += n


def _in_proj(x, mod_l, norm_g, prm, l):
    B, S, D = x.shape
    tm = _tile(S, TM_DENSE)
    n_in = prm["w_in"].shape[-1]
    tok = lambda n: pl.BlockSpec((None, tm, n), lambda b, s: (b, s, 0))
    vec = lambda: pl.BlockSpec((None, 1, WIDTH), lambda b, s: (l, 0, 0))
    mat = lambda r: pl.BlockSpec((None, r, WIDTH), lambda b, s: (l, 0, 0))
    return pl.pallas_call(
        functools.partial(_in_kernel, d=D, tm=tm),
        grid=(B, S // tm),
        in_specs=[
            tok(D),
            pl.BlockSpec((None, 1, 6 * D), lambda b, s: (b, 0, 0)),
            pl.BlockSpec((None, 1, D), lambda b, s: (l, 0, 0)),
            pl.BlockSpec((None, D, n_in), lambda b, s: (l, 0, 0), pipeline_mode=pl.Buffered(1)),
            mat(4), vec(), mat(WIDTH), vec(), mat(WIDTH), vec(), vec(),
        ],
        out_specs=[tok(WIDTH)] + [tok(n) for _, n in IN_COLS],
        out_shape=[jax.ShapeDtypeStruct((B, S, WIDTH), F32)]
        + [jax.ShapeDtypeStruct((B, S, n), F32) for _, n in IN_COLS],
        scratch_shapes=[pltpu.VMEM((SUBLANES, WIDTH), F32), pltpu.VMEM((SUBLANES, WIDTH), F32)],
        compiler_params=pltpu.CompilerParams(
            dimension_semantics=("parallel", "arbitrary"), vmem_limit_bytes=VMEM_LIMIT),
        name="in_proj_lru",
    )(x, mod_l, norm_g, prm["w_in"], prm["lru_conv_w"], prm["lru_conv_b"], prm["lru_wr"],
      prm["lru_b_r"], prm["lru_wi"], prm["lru_b_i"], prm["lru_lambda"])


def _sb_kernel(q_ref, k_ref, v_ref, o_ref, acc_ref, *, tq):
    i = pl.program_id(1)
    q = q_ref[...] * (HEAD_DIM ** -0.5)
    head = _iota2((1, WIDTH), 1) >> 6
    hm = [head == h for h in range(N_HEADS)]
    qh = [jnp.where(hm[h], q, 0.0).astype(MXU_DTYPE) for h in range(N_HEADS)]
    row = _iota2((tq, tq), 0)
    col = _iota2((tq, tq), 1)
    rev_incl = (row >= col).astype(jnp.bfloat16)
    past = col < row
    acc_ref[...] = jnp.zeros_like(acc_ref)

    def tiles(first, count, carries, diag):
        heads = range(N_HEADS)
        group = range(count)
        units = [(g, h) for g in group for h in heads]
        starts = [pl.multiple_of((first - g) * tq, tq) for g in group]
        kts = [k_ref[pl.ds(s, tq), :].astype(MXU_DTYPE) for s in starts]
        vts = [v_ref[pl.ds(s, tq), :].astype(MXU_DTYPE) for s in starts]
        zs = {(g, h): lax.dot_general(qh[h], kts[g], (((1,), (1,)), ((), ())),
                                      preferred_element_type=F32) for g, h in units}
        sps = {u: jnp.maximum(zs[u], 0.0) + jnp.log(1.0 + jnp.exp2(jnp.abs(zs[u]) * -LOG2_E))
               for u in units}
        if diag:
            sps.update({(0, h): jnp.where(past, sps[(0, h)], 0.0) for h in heads})
        part = {u: _dot_x01(sps[u], rev_incl, 2) for u in units}
        css = {}
        new = []
        for h in heads:
            c = carries[h]
            for g in group:
                css[(g, h)] = part[(g, h)] + c
                c = css[(g, h)][:, 0:1]
            new.append(c)
        ws = {u: jnp.exp(zs[u] - css[u]) for u in units}
        if diag:
            ws.update({(0, h): jnp.where(past, ws[(0, h)], 0.0) for h in heads})
        total = None
        for g in group:
            pv = None
            for h in heads:
                o = _dot(ws[(g, h)], vts[g])
                pv = o if pv is None else jnp.where(hm[h], o, pv)
            total = pv if total is None else total + pv
        acc_ref[...] += total
        return tuple(new)

    zero = (jnp.zeros((tq, 1), F32),) * N_HEADS
    extra = i % SB_GROUP
    carries = lax.switch(extra, [functools.partial(tiles, i, n + 1, zero, True)
                                 for n in range(SB_GROUP)])
    lax.fori_loop(0, i // SB_GROUP,
                  lambda it, c: tiles(i - extra - 1 - it * SB_GROUP, SB_GROUP, c, False), carries)
    o_ref[...] = acc_ref[...]


def _sb(p_sb):
    B, S, _ = p_sb.shape
    tq = _tile(S, TQ_SB)
    return pl.pallas_call(
        functools.partial(_sb_kernel, tq=tq),
        grid=(B, S // tq),
        in_specs=[
            pl.BlockSpec((None, tq, WIDTH), lambda b, i: (b, i, 0)),
            pl.BlockSpec((None, S, WIDTH), lambda b, i: (b, 0, 1)),
            pl.BlockSpec((None, S, WIDTH), lambda b, i: (b, 0, 2)),
        ],
        out_specs=pl.BlockSpec((None, tq, WIDTH), lambda b, i: (b, i, 0)),
        out_shape=jax.ShapeDtypeStruct((B, S, WIDTH), F32),
        scratch_shapes=[pltpu.VMEM((tq, WIDTH), F32)],
        compiler_params=pltpu.CompilerParams(
            dimension_semantics=("parallel", "arbitrary"), vmem_limit_bytes=VMEM_LIMIT),
        name="stick_breaking",
    )(p_sb, p_sb, p_sb)


def _gdn_kernel(p_ref, ab_ref, cw_ref, alog_ref, dtb_ref, ng_ref, o_ref,
                halo_ref, st_ref, *, ts):
    @pl.when(pl.program_id(1) == 0)
    def _():
        halo_ref[...] = jnp.zeros_like(halo_ref)
        st_ref[...] = jnp.zeros_like(st_ref)

    qkv = p_ref[:, 0:3 * WIDTH]
    halo = halo_ref[...]
    cw = cw_ref[...]
    c = (cw[3:4] * qkv + cw[2:3] * _shift_rows(qkv, 1, halo)
         + cw[1:2] * _shift_rows(qkv, 2, halo) + cw[0:1] * _shift_rows(qkv, 3, halo))
    halo_ref[...] = qkv[ts - SUBLANES:ts]
    c = _silu(c)
    ones_bd = _head_ones()
    q = c[:, 0:WIDTH]
    k = c[:, WIDTH:2 * WIDTH]
    qn = q * lax.rsqrt(_head_sum(q * q, ones_bd) + EPS) * (HEAD_DIM ** -0.5)
    kn = k * lax.rsqrt(_head_sum(k * k, ones_bd) + EPS)
    v = c[:, 2 * WIDTH:3 * WIDTH]
    ab = ab_ref[...]
    g = -jnp.exp(alog_ref[...]) * _softplus(ab + dtb_ref[...])
    beta = _sigmoid(ab)

    gexp = _expand_heads(g, 0)
    bx = _expand_heads(beta, N_HEADS)

    m = _sbs_masks()
    same = m["same_w"]
    cum_incl = (_iota2((CHUNK, CHUNK), 0) >= _iota2((CHUNK, CHUNK), 1)).astype(jnp.bfloat16)
    ones_cc = jnp.ones((CHUNK, CHUNK), jnp.bfloat16)
    chunks = range(ts // CHUNK)
    rows = [slice(ci * CHUNK, (ci + 1) * CHUNK) for ci in chunks]
    lo, hi = slice(0, CHUNK), slice(CHUNK, 2 * CHUNK)
    left, right = slice(0, STACK), slice(STACK, 2 * STACK)

    gx = [_dot_01x(cum_incl, gexp[r], 3) for r in rows]
    g_row = [_dot_01x(ones_cc, jnp.where(m["eye"], x, 0.0), 3) for x in gx]
    decay = [jnp.exp(jnp.where(m["incl"], gx[i] - g_row[i], -1e30)) for i in chunks]
    k_beta = [kn[rows[i]] * bx[rows[i]] for i in chunks]
    kq = [_dot_nt(jnp.concatenate([k_beta[i], qn[rows[i]]], axis=0), _bd(kn[rows[i]], same))
          for i in chunks]
    a = [jnp.where(m["strict"], kq[i][lo] * decay[i], 0.0) for i in chunks]
    qk = [jnp.where(m["incl"], kq[i][hi] * decay[i], 0.0) for i in chunks]
    t = _inv_unit_lower(a, m)
    eg = [jnp.exp(x) for x in gx]
    sol = [_dot(t[i], jnp.concatenate([_bd(v[rows[i]] * bx[rows[i]], same),
                                       _bd(k_beta[i] * eg[i], same)], axis=1))
           for i in chunks]
    gl = [x[CHUNK - 1:CHUNK] for x in gx]
    k_dec = [kn[rows[i]] * jnp.exp(gl[i] - gx[i]) for i in chunks]
    nm = [_dot_tn(sol[i], k_dec[i]) for i in chunks]
    ut_kd = [_diag_blocks(x[left]) for x in nm]
    wt_kd = [_bd(_diag_blocks(x[right]), same) for x in nm]
    wq = [jnp.concatenate([sol[i][:, right], qn[rows[i]] * eg[i]], axis=0) for i in chunks]

    state = st_ref[...]
    ws, outs = [], []

    def finish(i):
        u = sol[i][:, left] - ws[i][lo]
        outs.append(ws[i][hi] + _dot(qk[i], _bd(u, same)))

    for i in chunks:
        prev = state
        state = state * jnp.exp(gl[i]) - _dot(state, wt_kd[i]) + ut_kd[i]
        ws.append(_dot_nt(wq[i], _bd(prev, same)))
        if i > 0:
            finish(i - 1)
    finish(chunks[-1])
    st_ref[...] = state
    o = jnp.concatenate(outs, axis=0)
    o = o * lax.rsqrt(_head_sum(o * o, ones_bd) * (1.0 / HEAD_DIM) + EPS) * ng_ref[...]
    o_ref[...] = o * _silu(p_ref[:, 3 * WIDTH:4 * WIDTH])


def _gdn(p_gdn, p_ab, prm, l):
    B, S, _ = p_gdn.shape
    ts = _tile(S, TS_REC)
    return pl.pallas_call(
        functools.partial(_gdn_kernel, ts=ts),
        grid=(B, S // ts),
        in_specs=[
            pl.BlockSpec((None, ts, 4 * WIDTH), lambda b, s: (b, s, 0)),
            pl.BlockSpec((None, ts, LANES), lambda b, s: (b, s, 0)),
            pl.BlockSpec((None, 4, 3 * WIDTH), lambda b, s: (l, 0, 0)),
            pl.BlockSpec((None, 1, LANES), lambda b, s: (l, 0, 0)),
            pl.BlockSpec((None, 1, LANES), lambda b, s: (l, 0, 0)),
            pl.BlockSpec((None, 1, WIDTH), lambda b, s: (l, 0, 0)),
        ],
        out_specs=pl.BlockSpec((None, ts, WIDTH), lambda b, s: (b, s, 0)),
        out_shape=jax.ShapeDtypeStruct((B, S, WIDTH), F32),
        scratch_shapes=[pltpu.VMEM((SUBLANES, 3 * WIDTH), F32), pltpu.VMEM((CHUNK, WIDTH), F32)],
        compiler_params=pltpu.CompilerParams(
            dimension_semantics=("parallel", "arbitrary"), vmem_limit_bytes=VMEM_LIMIT),
        name="gated_deltanet",
    )(p_gdn, p_ab, prm["gdn_conv_w"], prm["gdn_a_log"], prm["gdn_dt_bias"], prm["gdn_norm_g"])


def _rw_kernel(p_ref, mu_ref, w0_ref, wup_ref, a0_ref, aup_ref, gup_ref, kk_ref, ka_ref, rk_ref,
               lng_ref, lnb_ref, o_ref,
               halo_ref, st_ref, *, ts):
    @pl.when(pl.program_id(1) == 0)
    def _():
        halo_ref[...] = jnp.zeros_like(halo_ref)
        st_ref[...] = jnp.zeros_like(st_ref)

    p = p_ref[...]
    prev = _shift_rows(p, 1, halo_ref[...])
    halo_ref[...] = p[ts - SUBLANES:ts]
    pf = p + (prev - p) * mu_ref[...]
    r = pf[:, 0:WIDTH]
    k = pf[:, WIDTH:2 * WIDTH]
    v = pf[:, 2 * WIDTH:3 * WIDTH]
    x = pf[:, 3 * WIDTH:3 * WIDTH + LANES]
    lw = _dot(jnp.tanh(x), wup_ref[...])
    la = _dot(x, aup_ref[...])
    gate = _dot(_sigmoid(x), gup_ref[...])
    w_log = -_softplus(-(w0_ref[...] + lw)) - 0.5
    a = _sigmoid(a0_ref[...] + la)
    ones_bd = _head_ones()
    kkv = k * kk_ref[...]
    kk = kkv * lax.rsqrt(_head_sum(kkv * kkv, ones_bd) + EPS)
    k = k * (1.0 + (a - 1.0) * ka_ref[...])
    b = kk * a
    ld = -jnp.exp(w_log)

    m = _sbs_masks()
    same = m["same_w"]
    cum_incl = (_iota2((CHUNK, CHUNK), 0) >= _iota2((CHUNK, CHUNK), 1)).astype(jnp.bfloat16)
    chunks = range(ts // CHUNK)
    rows = [slice(ci * CHUNK, (ci + 1) * CHUNK) for ci in chunks]
    bd = lambda t: _bd(t, same)
    lo, hi = slice(0, CHUNK), slice(CHUNK, 2 * CHUNK)
    left, right = slice(0, STACK), slice(STACK, 2 * STACK)

    lg = [_dot_01x(cum_incl, ld[r_], 2) for r_ in rows]
    ll = [x[CHUNK - 1:CHUNK] for x in lg]
    e_neg = [jnp.exp(-x) for x in lg]
    e_rest = [jnp.exp(ll[i] - lg[i]) for i in chunks]
    kt = [kk[rows[i]] * jnp.exp(lg[i] - ld[rows[i]]) for i in chunks]
    rt = [r[rows[i]] * jnp.exp(lg[i]) for i in chunks]
    kh = [k[rows[i]] * e_neg[i] for i in chunks]
    bh = [b[rows[i]] * e_neg[i] for i in chunks]
    kbar = [k[rows[i]] * e_rest[i] for i in chunks]
    bbar = [b[rows[i]] * e_rest[i] for i in chunks]
    big = [_dot_nt(jnp.concatenate([kt[i], rt[i]], axis=0),
                   jnp.concatenate([bd(bh[i]), bd(kh[i])], axis=0)) for i in chunks]
    a_kb = [jnp.where(m["strict"], x[lo, left], 0.0) for x in big]
    a_kk = [jnp.where(m["strict"], x[lo, right], 0.0) for x in big]
    a_rb = [jnp.where(m["incl"], x[hi, left], 0.0) for x in big]
    a_rk = [jnp.where(m["incl"], x[hi, right], 0.0) for x in big]
    t = _inv_unit_lower(a_kb, m)
    akv = [_dot(jnp.concatenate([a_kk[i], a_rk[i]], axis=0), bd(v[rows[i]])) for i in chunks]
    tk = [_dot(t[i], jnp.concatenate([bd(akv[i][lo]), bd(kt[i])], axis=1)) for i in chunks]
    tb = [_dot_tn(tk[i], bbar[i]) for i in chunks]
    ub = [_diag_blocks(x[left]) for x in tb]
    wb = [bd(_diag_blocks(x[right])) for x in tb]
    vk = [_diag_blocks(_dot_tn(v[rows[i]], kbar[i])) for i in chunks]
    wr = [jnp.concatenate([tk[i][:, right], rt[i]], axis=0) for i in chunks]

    state = st_ref[...]
    ws, outs = [], []

    def finish(i):
        u = ws[i][lo] + tk[i][:, left]
        outs.append(ws[i][hi] + akv[i][hi] - _dot(a_rb[i], bd(u)))

    for i in chunks:
        prev = state
        state = state * jnp.exp(ll[i]) - _dot(state, wb[i]) + (vk[i] - ub[i])
        ws.append(_dot_nt(wr[i], bd(prev)))
        if i > 0:
            finish(i - 1)
    finish(chunks[-1])
    st_ref[...] = state
    y = jnp.concatenate(outs, axis=0)
    mean =_head_sum(y, ones_bd) * (1.0 / HEAD_DIM)
    yc = y - mean
    var = _head_sum(yc * yc, ones_bd) * (1.0 / HEAD_DIM)
    yn = yc * lax.rsqrt(var + RW_LN_EPS) * lng_ref[...] + lnb_ref[...]
    bonus = _head_sum(r * k * rk_ref[...], ones_bd) * v
    o_ref[...] = (yn + bonus) * gate


def _rw(p_rw, prm, l):
    B, S, n_rw = p_rw.shape
    ts = _tile(S, TS_REC)
    vec = lambda n=WIDTH: pl.BlockSpec((None, 1, n), lambda b, s: (l, 0, 0))
    lora = lambda: pl.BlockSpec((None, LANES, WIDTH), lambda b, s: (l, 0, 0))
    return pl.pallas_call(
        functools.partial(_rw_kernel, ts=ts),
        grid=(B, S // ts),
        in_specs=[
            pl.BlockSpec((None, ts, n_rw), lambda b, s: (b, s, 0)),
            vec(n_rw), vec(), lora(), vec(), lora(), lora(), vec(), vec(), vec(), vec(), vec(),
        ],
        out_specs=pl.BlockSpec((None, ts, WIDTH), lambda b, s: (b, s, 0)),
        out_shape=jax.ShapeDtypeStruct((B, S, WIDTH), F32),
        scratch_shapes=[pltpu.VMEM((SUBLANES, n_rw), F32), pltpu.VMEM((CHUNK, WIDTH), F32)],
        compiler_params=pltpu.CompilerParams(
            dimension_semantics=("parallel", "arbitrary"), vmem_limit_bytes=VMEM_LIMIT),
        name="rwkv7",
    )(p_rw, prm["rw_mu"], prm["rw_w0"], prm["rw_wup"], prm["rw_a0"], prm["rw_aup"], prm["rw_gup"],
      prm["rw_k_k"], prm["rw_k_a"], prm["rw_r_k"], prm["rw_ln_g"], prm["rw_ln_b"])


def _out_kernel(x_ref, mod_ref, g_ref, o0_ref, o1_ref, o2_ref, o3_ref,
                wg_ref, bg_ref, wb_ref, wo_ref, xo_ref, *, d):
    x = x_ref[...]
    mod = mod_ref[...]
    h = _modulated_norm(x, g_ref[...], mod[:, d:2 * d], mod[:, 0:d]).astype(MXU_DTYPE)
    mixed = None
    for n, o_ref in enumerate((o0_ref, o1_ref, o2_ref, o3_ref)):
        gate = _sigmoid(jnp.dot(h, wg_ref[n], preferred_element_type=F32) + bg_ref[n])
        y = jnp.dot(o_ref[...].astype(MXU_DTYPE), wb_ref[n], preferred_element_type=F32)
        mixed = gate * y if mixed is None else mixed + gate * y
    out = jnp.dot(mixed.astype(MXU_DTYPE), wo_ref[...], preferred_element_type=F32)
    xo_ref[...] = x + mod[:, 2 * d:3 * d] * out


def _out_proj(x, mod_l, norm_g, branches, prm, l):
    B, S, D = x.shape
    tm = _tile(S, TM_DENSE)
    tok = lambda n: pl.BlockSpec((None, tm, n), lambda b, s: (b, s, 0))
    const = lambda shape: pl.BlockSpec((None,) + shape, lambda b, s: (l,) + (0,) * len(shape),
                                       pipeline_mode=pl.Buffered(1))
    return pl.pallas_call(
        functools.partial(_out_kernel, d=D),
        grid=(B, S // tm),
        in_specs=[
            tok(D),
            pl.BlockSpec((None, 1, 6 * D), lambda b, s: (b, 0, 0)),
            const((1, D)),
            tok(WIDTH), tok(WIDTH), tok(WIDTH), tok(WIDTH),
            const((N_HEADS, D, D)), const((N_HEADS, 1, D)), const((N_HEADS, WIDTH, D)), const((D, D)),
        ],
        out_specs=tok(D),
        out_shape=jax.ShapeDtypeStruct((B, S, D), F32),
        compiler_params=pltpu.CompilerParams(
            dimension_semantics=("parallel", "parallel"), vmem_limit_bytes=VMEM_LIMIT),
        name="out_proj",
    )(x, mod_l, norm_g, *branches, prm["w_gate"], prm["b_gate"], prm["w_branch"], prm["w_out"])


def _ffn_kernel(x_ref, mod_ref, g_ref, wg_ref, wu_ref, cw_ref, wd_ref, fg_ref, xo_ref, halo_ref,
                *, d, tm, fc, final):
    @pl.when(pl.program_id(1) == 0)
    def _():
        halo_ref[...] = jnp.zeros_like(halo_ref)

    x = x_ref[...]
    mod = mod_ref[...]
    h = _modulated_norm(x, g_ref[...], mod[:, 4 * d:5 * d], mod[:, 3 * d:4 * d]).astype(MXU_DTYPE)
    acc = None
    for j in range(FFN_SPLIT):
        cols = slice(j * fc, (j + 1) * fc)
        a_pre = jnp.dot(h, wg_ref[:, cols], preferred_element_type=F32)
        up = jnp.dot(h, wu_ref[:, cols], preferred_element_type=F32)
        halo = halo_ref[:, cols]
        cw = cw_ref[:, cols]
        a = (cw[2:3] * a_pre + cw[1:2] * _shift_rows(a_pre, 1, halo)
             + cw[0:1] * _shift_rows(a_pre, 2, halo))
        halo_ref[:, cols] = a_pre[tm - SUBLANES:tm]
        act = (_gelu_tanh(a) * up).astype(MXU_DTYPE)
        part = jnp.dot(act, wd_ref[cols, :], preferred_element_type=F32)
        acc = part if acc is None else acc + part
    y = x + mod[:, 5 * d:6 * d] * acc
    if final:
        y = y * lax.rsqrt(jnp.mean(y * y, axis=-1, keepdims=True) + EPS) * fg_ref[...]
    xo_ref[...] = y


def _ffn(x, mod_l, norm_g, prm, final_g, l, final):
    B, S, D = x.shape
    F = prm["ffn_w_gate"].shape[-1]
    tm = _tile(S, TM_DENSE)
    fc = F // FFN_SPLIT
    tok = lambda n: pl.BlockSpec((None, tm, n), lambda b, s: (b, s, 0))
    const = lambda shape: pl.BlockSpec((None,) + shape, lambda b, s: (l,) + (0,) * len(shape),
                                       pipeline_mode=pl.Buffered(1))
    return pl.pallas_call(
        functools.partial(_ffn_kernel, d=D, tm=tm, fc=fc, final=final),
        grid=(B, S // tm),
        in_specs=[
            tok(D),
            pl.BlockSpec((None, 1, 6 * D), lambda b, s: (b, 0, 0)),
            const((1, D)),
            const((D, F)), const((D, F)), const((3, F)), const((F, D)),
            pl.BlockSpec((1, D), lambda b, s: (0, 0)),
        ],
        out_specs=tok(D),
        out_shape=jax.ShapeDtypeStruct((B, S, D), F32),
        scratch_shapes=[pltpu.VMEM((SUBLANES, F), F32)],
        compiler_params=pltpu.CompilerParams(
            dimension_semantics=("parallel", "arbitrary"), vmem_limit_bytes=VMEM_LIMIT),
        name="conv_ffn",
    )(x, mod_l, norm_g, prm["ffn_w_gate"], prm["ffn_w_up"], prm["ffn_conv_w"], prm["ffn_w_down"],
      final_g)


def _prepare(w_in, lru_conv_w, lru_conv_b, lru_w_r, lru_b_r, lru_w_i, lru_b_i, lru_lambda,
             gdn_conv_w, gdn_a_log, gdn_dt_bias, gdn_norm_g,
             rw_mu, rw_w0, rw_w_up, rw_a0, rw_a_up, rw_g_up, rw_k_k, rw_k_a, rw_r_k, rw_ln_g, rw_ln_b,
             w_branch, w_gate, b_gate, w_out, ffn_w_gate, ffn_w_up, ffn_conv_w, ffn_w_down):
    L, D, _ = w_in.shape
    bf = lambda t: t.astype(MXU_DTYPE)
    row = lambda t: t.reshape(L, 1, -1)

    n_ab = 2 * N_HEADS
    c_ab = 512 + 768 + 1024
    w_in_p = jnp.concatenate(
        [w_in[:, :, :c_ab], w_in[:, :, c_ab:c_ab + n_ab],
         jnp.zeros((L, D, LANES - n_ab), w_in.dtype), w_in[:, :, c_ab + n_ab:]], axis=-1)

    def block_diag(w):
        eye = jnp.eye(N_HEADS, dtype=w.dtype)
        return jnp.einsum("lnef,nm->lnemf", w, eye).reshape(L, WIDTH, WIDTH)

    def lane_pad(t, lo):
        return jnp.pad(t, ((0, 0), (lo, LANES - lo - t.shape[1]))).reshape(L, 1, LANES)

    def lora_pad(w, lo):
        return jnp.pad(w, ((0, 0), (lo, LANES - lo - w.shape[1]), (0, 0)))

    return dict(
        w_in=bf(w_in_p),
        lru_conv_w=lru_conv_w, lru_conv_b=row(lru_conv_b),
        lru_wr=bf(block_diag(lru_w_r)), lru_b_r=row(lru_b_r),
        lru_wi=bf(block_diag(lru_w_i)), lru_b_i=row(lru_b_i), lru_lambda=row(lru_lambda),
        gdn_conv_w=gdn_conv_w, gdn_a_log=lane_pad(gdn_a_log, 0), gdn_dt_bias=lane_pad(gdn_dt_bias, 0),
        gdn_norm_g=row(jnp.tile(gdn_norm_g, (1, N_HEADS))),
        rw_mu=row(rw_mu), rw_w0=row(rw_w0), rw_a0=row(rw_a0),
        rw_wup=bf(lora_pad(rw_w_up, 0)), rw_aup=bf(lora_pad(rw_a_up, W_LORA)),
        rw_gup=bf(lora_pad(rw_g_up, W_LORA + A_LORA)),
        rw_k_k=row(rw_k_k), rw_k_a=row(rw_k_a), rw_r_k=row(rw_r_k),
        rw_ln_g=row(rw_ln_g), rw_ln_b=row(rw_ln_b),
        w_gate=bf(w_gate), b_gate=b_gate.reshape(L, N_HEADS, 1, D), w_branch=bf(w_branch), w_out=bf(w_out),
        ffn_w_gate=bf(ffn_w_gate), ffn_w_up=bf(ffn_w_up), ffn_conv_w=ffn_conv_w, ffn_w_down=bf(ffn_w_down),
    )


def kernel(x, c, norm1_g, norm2_g, final_g, w_ada, b_ada, w_in, lru_conv_w, lru_conv_b, lru_w_r, lru_b_r, lru_w_i, lru_b_i, lru_lambda, gdn_conv_w, gdn_a_log, gdn_dt_bias, gdn_norm_g, rw_mu, rw_w0, rw_w_up, rw_a0, rw_a_up, rw_g_up, rw_k_k, rw_k_a, rw_r_k, rw_ln_g, rw_ln_b, w_branch, w_gate, b_gate, w_out, ffn_w_gate, ffn_w_up, ffn_conv_w, ffn_w_down):
    B, S, D = x.shape
    L = w_in.shape[0]
    prm = _prepare(w_in, lru_conv_w, lru_conv_b, lru_w_r, lru_b_r, lru_w_i, lru_b_i, lru_lambda,
                   gdn_conv_w, gdn_a_log, gdn_dt_bias, gdn_norm_g,
                   rw_mu, rw_w0, rw_w_up, rw_a0, rw_a_up, rw_g_up, rw_k_k, rw_k_a, rw_r_k,
                   rw_ln_g, rw_ln_b, w_branch, w_gate, b_gate, w_out,
                   ffn_w_gate, ffn_w_up, ffn_conv_w, ffn_w_down)
    mod = _modulation(c, w_ada, b_ada).reshape(L, B, 1, 6 * D)
    n1 = norm1_g.reshape(L, 1, D)
    n2 = norm2_g.reshape(L, 1, D)
    fg = final_g.reshape(1, D)
    for l in range(L):
        o_lru, p_sb, p_gdn, p_ab, p_rw = _in_proj(x, mod[l], n1, prm, l)
        branches = (o_lru, _sb(p_sb), _gdn(p_gdn, p_ab, prm, l), _rw(p_rw, prm, l))
        x = _out_proj(x, mod[l], n1, branches, prm, l)
        x = _ffn(x, mod[l], n2, prm, fg, l, final=(l == L - 1))
    return x
```

```python
import functools

import jax
import jax.numpy as jnp
from jax import lax
from jax.experimental import pallas as pl
from jax.experimental.pallas import tpu as pltpu

F32 = jnp.float32
MXU_DTYPE = jnp.bfloat16

HEAD_DIM = 64
N_HEADS = 4
WIDTH = N_HEADS * HEAD_DIM
CHUNK = 64
STACK = N_HEADS * CHUNK
SUBLANES = 8
LANES = 128
EPS = 1e-6
LOG2_E = 1.4426950408889634
RW_LN_EPS = 64e-5
LRU_C = 8.0
W_LORA, A_LORA, G_LORA = 32, 32, 64
FFN_SPLIT = 2

TM_DENSE = 1024
TS_REC = 1024
TQ_SB = 256
SB_GROUP = 4

VMEM_LIMIT = 56 * 1024 * 1024


def _tile(seq, preferred):
    tile = min(preferred, seq)
    assert seq % tile == 0 and tile % CHUNK == 0, (seq, tile)
    return tile


def _dot(a, b):
    return lax.dot_general(a, b.astype(MXU_DTYPE), (((1,), (0,)), ((), ())),
                           preferred_element_type=F32)


def _dot_nt(a, b):
    return lax.dot_general(a.astype(MXU_DTYPE), b.astype(MXU_DTYPE),
                           (((1,), (1,)), ((), ())), preferred_element_type=F32)


def _dot_tn(a, b):
    return lax.dot_general(a.astype(MXU_DTYPE), b.astype(MXU_DTYPE),
                           (((0,), (0,)), ((), ())), preferred_element_type=F32)


def _split(x, parts):
    out = []
    r = x
    for _ in range(parts - 1):
        p = r.astype(jnp.bfloat16)
        out.append(p)
        r = r - p.astype(F32)
    out.append(r.astype(jnp.bfloat16))
    return out


def _dot_x01(x, m01, parts):
    acc = None
    for p in _split(x, parts):
        t = jnp.dot(p, m01, preferred_element_type=F32)
        acc = t if acc is None else acc + t
    return acc


def _dot_01x(m01, x, parts):
    acc = None
    for p in _split(x, parts):
        t = jnp.dot(m01, p, preferred_element_type=F32)
        acc = t if acc is None else acc + t
    return acc


def _sigmoid(x):
    return jax.nn.sigmoid(x)


def _softplus(x):
    return jnp.maximum(x, 0.0) + jnp.log1p(jnp.exp(-jnp.abs(x)))


def _silu(x):
    return x * _sigmoid(x)


def _gelu_tanh(x):
    return 0.5 * x * (1.0 + jnp.tanh(0.7978845608028654 * (x + 0.044715 * (x * x * x))))


def _iota2(shape, dim):
    return lax.broadcasted_iota(jnp.int32, shape, dim)


def _shift_rows(x, k, halo):
    r = pltpu.roll(x, k, 0)
    hr = pltpu.roll(halo, k, 0)
    first = jnp.where(_iota2(halo.shape, 0) < k, hr, r[0:SUBLANES])
    return jnp.concatenate([first, r[SUBLANES:]], axis=0)


def _head_ones():
    return ((_iota2((WIDTH, WIDTH), 0) >> 6) == (_iota2((WIDTH, WIDTH), 1) >> 6)).astype(jnp.bfloat16)


def _head_sum(x, ones_bd):
    return _dot_x01(x, ones_bd, 2)


def _sbs_masks():
    t = _iota2((CHUNK, WIDTH), 0)
    j = _iota2((CHUNK, WIDTH), 1) & (CHUNK - 1)
    r = _iota2((STACK, STACK), 0)
    c = _iota2((STACK, STACK), 1)
    return dict(
        incl=t >= j,
        strict=t > j,
        m16=(t >> 4) == (j >> 4),
        m32=(t >> 5) == (j >> 5),
        eye=t == j,
        same_w=((r >> 6) == (c >> 6)).astype(MXU_DTYPE),
    )


def _bd(x, same):
    return jnp.concatenate([x.astype(MXU_DTYPE)] * N_HEADS, axis=0) * same


def _diag_blocks(full):
    lane_head = _iota2((CHUNK, WIDTH), 1) >> 6
    out = full[0:CHUNK]
    for h in range(1, N_HEADS):
        out = jnp.where(lane_head == h, full[h * CHUNK:(h + 1) * CHUNK], out)
    return out


def _expand_heads(x, lane_base):
    e = (_iota2((LANES, WIDTH), 0) == (_iota2((LANES, WIDTH), 1) >> 6) + lane_base)
    return _dot_x01(x, e.astype(jnp.bfloat16), 2)


def _inv_unit_lower(mats, m):
    n = range(len(mats))
    same = m["same_w"]
    a0 = [jnp.where(m["m16"], a, 0.0) for a in mats]
    a1 = [jnp.where(m["m32"] & jnp.logical_not(m["m16"]), a, 0.0) for a in mats]
    a2 = [jnp.where(m["m32"], 0.0, a) for a in mats]
    p = [jnp.where(m["eye"], 1.0, 0.0) - a for a in a0]
    x = [_dot(a, _bd(a, same)) for a in a0]
    for _ in range(2):
        px = [_dot(jnp.concatenate([p[i], x[i]], axis=0), _bd(x[i], same)) for i in n]
        p = [p[i] + px[i][0:CHUNK] for i in n]
        x = [px[i][CHUNK:2 * CHUNK] for i in n]
    p = [p[i] + _dot(p[i], _bd(x[i], same)) for i in n]
    for off in (a1, a2):
        t = [_dot(p[i], _bd(off[i], same)) for i in n]
        p = [p[i] - _dot(t[i], _bd(p[i], same)) for i in n]
    return p


def _modulated_norm(x, g, scale, shift):
    ms = jnp.mean(x * x, axis=-1, keepdims=True)
    return (x * lax.rsqrt(ms + EPS) * g) * (1.0 + scale) + shift


def _mod_kernel(c_ref, w_ref, b_ref, o_ref):
    c = c_ref[...]
    cond = c * _sigmoid(c)
    o_ref[...] = jnp.dot(cond, w_ref[...], precision=lax.Precision.HIGHEST,
                         preferred_element_type=F32) + b_ref[...]


def _modulation(c, w_ada, b_ada):
    L, D, D6 = w_ada.shape
    B = c.shape[0]
    nj = D6 // D
    return pl.pallas_call(
        _mod_kernel,
        grid=(L, nj),
        in_specs=[
            pl.BlockSpec((B, D), lambda l, j: (0, 0)),
            pl.BlockSpec((None, D, D), lambda l, j: (l, 0, j)),
            pl.BlockSpec((None, 1, D), lambda l, j: (l, 0, j)),
        ],
        out_specs=pl.BlockSpec((None, B, D), lambda l, j: (l, 0, j)),
        out_shape=jax.ShapeDtypeStruct((L, B, D6), F32),
        name="adaln_mod",
    )(c, w_ada, b_ada.reshape(L, 1, D6))


LRU_COLS = 2 * WIDTH
IN_COLS = (("sb", 768), ("gdn", 1024), ("ab", LANES), ("rw", 896))


def _rg_lru(p, cw, cb, wr, br, wi, bi, lam, halo_ref, h_ref, ts):
    x_in = p[:, 0:WIDTH]
    y_in = p[:, WIDTH:2 * WIDTH]
    halo = halo_ref[...]
    u = (cw[3:4] * x_in + cw[2:3] * _shift_rows(x_in, 1, halo)
         + cw[1:2] * _shift_rows(x_in, 2, halo) + cw[0:1] * _shift_rows(x_in, 3, halo) + cb)
    halo_ref[...] = x_in[ts - SUBLANES:ts]

    ub = u.astype(MXU_DTYPE)
    r = _sigmoid(jnp.dot(ub, wr, preferred_element_type=F32) + br)
    i = _sigmoid(jnp.dot(ub, wi, preferred_element_type=F32) + bi)
    log_a = (-LRU_C * r) * _softplus(-lam)
    a = jnp.exp(log_a)
    bv = jnp.sqrt(-jnp.tanh(log_a) * (a * a + 1.0)) * (i * u)

    low = _iota2((SUBLANES, WIDTH), 0)
    carry = h_ref[SUBLANES - 1:SUBLANES]
    groups = []
    for g in range(ts // SUBLANES):
        ag = a[g * SUBLANES:(g + 1) * SUBLANES]
        bg = bv[g * SUBLANES:(g + 1) * SUBLANES]
        for d in (1, 2, 4):
            bg = ag * jnp.where(low < d, 0.0, pltpu.roll(bg, d, 0)) + bg
            ag = ag * jnp.where(low < d, 1.0, pltpu.roll(ag, d, 0))
        hg = bg + ag * carry
        carry = hg[SUBLANES - 1:SUBLANES]
        groups.append(hg)
    h = jnp.concatenate(groups, axis=0)
    h_ref[...] = groups[-1]
    return h * _gelu_tanh(y_in)


def _in_kernel(x_ref, mod_ref, g_ref, w_ref, cw_ref, cb_ref, wr_ref, br_ref, wi_ref, bi_ref,
               lam_ref, o_lru, *rest, d, tm):
    o_refs, (halo_ref, h_ref) = rest[:len(IN_COLS)], rest[len(IN_COLS):]

    @pl.when(pl.program_id(1) == 0)
    def _():
        halo_ref[...] = jnp.zeros_like(halo_ref)
        h_ref[...] = jnp.zeros_like(h_ref)

    mod = mod_ref[...]
    h = _modulated_norm(x_ref[...], g_ref[...], mod[:, d:2 * d], mod[:, 0:d]).astype(MXU_DTYPE)
    p_lru = jnp.dot(h, w_ref[:, 0:LRU_COLS], preferred_element_type=F32)
    o_lru[...] = _rg_lru(p_lru, cw_ref[...], cb_ref[...], wr_ref[...], br_ref[...], wi_ref[...],
                         bi_ref[...], lam_ref[...], halo_ref, h_ref, tm)
    c0 = LRU_COLS
    for (_, n), o_ref in zip(IN_COLS, o_refs):
        o_ref[...] = jnp.dot(h, w_ref[:, c0:c0 + n], preferred_element_type=F32)
        c0 += n


def _in_proj(x, mod_l, norm_g, prm, l):
    B, S, D = x.shape
    tm = _tile(S, TM_DENSE)
    n_in = prm["w_in"].shape[-1]
    tok = lambda n: pl.BlockSpec((None, tm, n), lambda b, s: (b, s, 0))
    vec = lambda: pl.BlockSpec((None, 1, WIDTH), lambda b, s: (l, 0, 0))
    mat = lambda r: pl.BlockSpec((None, r, WIDTH), lambda b, s: (l, 0, 0))
    return pl.pallas_call(
        functools.partial(_in_kernel, d=D, tm=tm),
        grid=(B, S // tm),
        in_specs=[
            tok(D),
            pl.BlockSpec((None, 1, 6 * D), lambda b, s: (b, 0, 0)),
            pl.BlockSpec((None, 1, D), lambda b, s: (l, 0, 0)),
            pl.BlockSpec((None, D, n_in), lambda b, s: (l, 0, 0), pipeline_mode=pl.Buffered(1)),
            mat(4), vec(), mat(WIDTH), vec(), mat(WIDTH), vec(), vec(),
        ],
        out_specs=[tok(WIDTH)] + [tok(n) for _, n in IN_COLS],
        out_shape=[jax.ShapeDtypeStruct((B, S, WIDTH), F32)]
        + [jax.ShapeDtypeStruct((B, S, n), F32) for _, n in IN_COLS],
        scratch_shapes=[pltpu.VMEM((SUBLANES, WIDTH), F32), pltpu.VMEM((SUBLANES, WIDTH), F32)],
        compiler_params=pltpu.CompilerParams(
            dimension_semantics=("parallel", "arbitrary"), vmem_limit_bytes=VMEM_LIMIT),
        name="in_proj_lru",
    )(x, mod_l, norm_g, prm["w_in"], prm["lru_conv_w"], prm["lru_conv_b"], prm["lru_wr"],
      prm["lru_b_r"], prm["lru_wi"], prm["lru_b_i"], prm["lru_lambda"])


def _sb_kernel(q_ref, k_ref, v_ref, o_ref, acc_ref, *, tq):
    i = pl.program_id(1)
    q = q_ref[...] * (HEAD_DIM ** -0.5)
    head = _iota2((1, WIDTH), 1) >> 6
    hm = [head == h for h in range(N_HEADS)]
    qh = [jnp.where(hm[h], q, 0.0).astype(MXU_DTYPE) for h in range(N_HEADS)]
    row = _iota2((tq, tq), 0)
    col = _iota2((tq, tq), 1)
    rev_incl = (row >= col).astype(jnp.bfloat16)
    past = col < row
    acc_ref[...] = jnp.zeros_like(acc_ref)

    def tiles(first, count, carries, diag):
        heads = range(N_HEADS)
        group = range(count)
        units = [(g, h) for g in group for h in heads]
        starts = [pl.multiple_of((first - g) * tq, tq) for g in group]
        kts = [k_ref[pl.ds(s, tq), :].astype(MXU_DTYPE) for s in starts]
        vts = [v_ref[pl.ds(s, tq), :].astype(MXU_DTYPE) for s in starts]
        zs = {(g, h): lax.dot_general(qh[h], kts[g], (((1,), (1,)), ((), ())),
                                      preferred_element_type=F32) for g, h in units}
        sps = {u: jnp.maximum(zs[u], 0.0) + jnp.log(1.0 + jnp.exp2(jnp.abs(zs[u]) * -LOG2_E))
               for u in units}
        if diag:
            sps.update({(0, h): jnp.where(past, sps[(0, h)], 0.0) for h in heads})
        part = {u: _dot_x01(sps[u], rev_incl, 2) for u in units}
        css = {}
        new = []
        for h in heads:
            c = carries[h]
            for g in group:
                css[(g, h)] = part[(g, h)] + c
                c = css[(g, h)][:, 0:1]
            new.append(c)
        ws = {u: jnp.exp(zs[u] - css[u]) for u in units}
        if diag:
            ws.update({(0, h): jnp.where(past, ws[(0, h)], 0.0) for h in heads})
        total = None
        for g in group:
            pv = None
            for h in heads:
                o = _dot(ws[(g, h)], vts[g])
                pv = o if pv is None else jnp.where(hm[h], o, pv)
            total = pv if total is None else total + pv
        acc_ref[...] += total
        return tuple(new)

    zero = (jnp.zeros((tq, 1), F32),) * N_HEADS
    extra = i % SB_GROUP
    carries = lax.switch(extra, [functools.partial(tiles, i, n + 1, zero, True)
                                 for n in range(SB_GROUP)])
    lax.fori_loop(0, i // SB_GROUP,
                  lambda it, c: tiles(i - extra - 1 - it * SB_GROUP, SB_GROUP, c, False), carries)
    o_ref[...] = acc_ref[...]


def _sb(p_sb):
    B, S, _ = p_sb.shape
    tq = _tile(S, TQ_SB)
    return pl.pallas_call(
        functools.partial(_sb_kernel, tq=tq),
        grid=(B, S // tq),
        in_specs=[
            pl.BlockSpec((None, tq, WIDTH), lambda b, i: (b, i, 0)),
            pl.BlockSpec((None, S, WIDTH), lambda b, i: (b, 0, 1)),
            pl.BlockSpec((None, S, WIDTH), lambda b, i: (b, 0, 2)),
        ],
        out_specs=pl.BlockSpec((None, tq, WIDTH), lambda b, i: (b, i, 0)),
        out_shape=jax.ShapeDtypeStruct((B, S, WIDTH), F32),
        scratch_shapes=[pltpu.VMEM((tq, WIDTH), F32)],
        compiler_params=pltpu.CompilerParams(
            dimension_semantics=("parallel", "arbitrary"), vmem_limit_bytes=VMEM_LIMIT),
        name="stick_breaking",
    )(p_sb, p_sb, p_sb)


def _gdn_kernel(p_ref, ab_ref, cw_ref, alog_ref, dtb_ref, ng_ref, o_ref,
                halo_ref, st_ref, *, ts):
    @pl.when(pl.program_id(1) == 0)
    def _():
        halo_ref[...] = jnp.zeros_like(halo_ref)
        st_ref[...] = jnp.zeros_like(st_ref)

    qkv = p_ref[:, 0:3 * WIDTH]
    halo = halo_ref[...]
    cw = cw_ref[...]
    c = (cw[3:4] * qkv + cw[2:3] * _shift_rows(qkv, 1, halo)
         + cw[1:2] * _shift_rows(qkv, 2, halo) + cw[0:1] * _shift_rows(qkv, 3, halo))
    halo_ref[...] = qkv[ts - SUBLANES:ts]
    c = _silu(c)
    ones_bd = _head_ones()
    q = c[:, 0:WIDTH]
    k = c[:, WIDTH:2 * WIDTH]
    qn = q * lax.rsqrt(_head_sum(q * q, ones_bd) + EPS) * (HEAD_DIM ** -0.5)
    kn = k * lax.rsqrt(_head_sum(k * k, ones_bd) + EPS)
    v = c[:, 2 * WIDTH:3 * WIDTH]
    ab = ab_ref[...]
    g = -jnp.exp(alog_ref[...]) * _softplus(ab + dtb_ref[...])
    beta = _sigmoid(ab)

    gexp = _expand_heads(g, 0)
    bx = _expand_heads(beta, N_HEADS)

    m = _sbs_masks()
    same = m["same_w"]
    cum_incl = (_iota2((CHUNK, CHUNK), 0) >= _iota2((CHUNK, CHUNK), 1)).astype(jnp.bfloat16)
    ones_cc = jnp.ones((CHUNK, CHUNK), jnp.bfloat16)
    lo, hi = slice(0, CHUNK), slice(CHUNK, 2 * CHUNK)
    left, right = slice(0, STACK), slice(STACK, 2 * STACK)
    outs = []

    def wave(rows, state):
        chunks = range(len(rows))
        gx = [_dot_01x(cum_incl, gexp[r], 3) for r in rows]
        g_row = [_dot_01x(ones_cc, jnp.where(m["eye"], x, 0.0), 3) for x in gx]
        decay = [jnp.exp(jnp.where(m["incl"], gx[i] - g_row[i], -1e30)) for i in chunks]
        k_beta = [kn[rows[i]] * bx[rows[i]] for i in chunks]
        kq = [_dot_nt(jnp.concatenate([k_beta[i], qn[rows[i]]], axis=0), _bd(kn[rows[i]], same))
              for i in chunks]
        a = [jnp.where(m["strict"], kq[i][lo] * decay[i], 0.0) for i in chunks]
        qk = [jnp.where(m["incl"], kq[i][hi] * decay[i], 0.0) for i in chunks]
        t = _inv_unit_lower(a, m)
        eg = [jnp.exp(x) for x in gx]
        sol = [_dot(t[i], jnp.concatenate([_bd(v[rows[i]] * bx[rows[i]], same),
                                           _bd(k_beta[i] * eg[i], same)], axis=1))
               for i in chunks]
        gl = [x[CHUNK - 1:CHUNK] for x in gx]
        k_dec = [kn[rows[i]] * jnp.exp(gl[i] - gx[i]) for i in chunks]
        nm = [_dot_tn(sol[i], k_dec[i]) for i in chunks]
        ut_kd = [_diag_blocks(x[left]) for x in nm]
        wt_kd = [_bd(_diag_blocks(x[right]), same) for x in nm]
        wq = [jnp.concatenate([sol[i][:, right], qn[rows[i]] * eg[i]], axis=0) for i in chunks]

        ws = []

        def finish(i):
            u = sol[i][:, left] - ws[i][lo]
            outs.append(ws[i][hi] + _dot(qk[i], _bd(u, same)))

        for i in chunks:
            prev = state
            state = state * jnp.exp(gl[i]) - _dot(state, wt_kd[i]) + ut_kd[i]
            ws.append(_dot_nt(wq[i], _bd(prev, same)))
            if i > 0:
                finish(i - 1)
        finish(chunks[-1])
        return state

    rows = [slice(ci * CHUNK, (ci + 1) * CHUNK) for ci in range(ts // CHUNK)]
    half = max(len(rows) // 2, 1)
    state = st_ref[...]
    for w0 in range(0, len(rows), half):
        state = wave(rows[w0:w0 + half], state)
    st_ref[...] = state
    o = jnp.concatenate(outs, axis=0)
    o = o * lax.rsqrt(_head_sum(o * o, ones_bd) * (1.0 / HEAD_DIM) + EPS) * ng_ref[...]
    o_ref[...] = o * _silu(p_ref[:, 3 * WIDTH:4 * WIDTH])


def _gdn(p_gdn, p_ab, prm, l):
    B, S, _ = p_gdn.shape
    ts = _tile(S, TS_REC)
    return pl.pallas_call(
        functools.partial(_gdn_kernel, ts=ts),
        grid=(B, S // ts),
        in_specs=[
            pl.BlockSpec((None, ts, 4 * WIDTH), lambda b, s: (b, s, 0)),
            pl.BlockSpec((None, ts, LANES), lambda b, s: (b, s, 0)),
            pl.BlockSpec((None, 4, 3 * WIDTH), lambda b, s: (l, 0, 0)),
            pl.BlockSpec((None, 1, LANES), lambda b, s: (l, 0, 0)),
            pl.BlockSpec((None, 1, LANES), lambda b, s: (l, 0, 0)),
            pl.BlockSpec((None, 1, WIDTH), lambda b, s: (l, 0, 0)),
        ],
        out_specs=pl.BlockSpec((None, ts, WIDTH), lambda b, s: (b, s, 0)),
        out_shape=jax.ShapeDtypeStruct((B, S, WIDTH), F32),
        scratch_shapes=[pltpu.VMEM((SUBLANES, 3 * WIDTH), F32), pltpu.VMEM((CHUNK, WIDTH), F32)],
        compiler_params=pltpu.CompilerParams(
            dimension_semantics=("parallel", "arbitrary"), vmem_limit_bytes=VMEM_LIMIT),
        name="gated_deltanet",
    )(p_gdn, p_ab, prm["gdn_conv_w"], prm["gdn_a_log"], prm["gdn_dt_bias"], prm["gdn_norm_g"])


def _rw_kernel(p_ref, mu_ref, w0_ref, wup_ref, a0_ref, aup_ref, gup_ref, kk_ref, ka_ref, rk_ref,
               lng_ref, lnb_ref, o_ref,
               halo_ref, st_ref, *, ts):
    @pl.when(pl.program_id(1) == 0)
    def _():
        halo_ref[...] = jnp.zeros_like(halo_ref)
        st_ref[...] = jnp.zeros_like(st_ref)

    p = p_ref[...]
    prev = _shift_rows(p, 1, halo_ref[...])
    halo_ref[...] = p[ts - SUBLANES:ts]
    pf = p + (prev - p) * mu_ref[...]
    r = pf[:, 0:WIDTH]
    k = pf[:, WIDTH:2 * WIDTH]
    v = pf[:, 2 * WIDTH:3 * WIDTH]
    x = pf[:, 3 * WIDTH:3 * WIDTH + LANES]
    lw = _dot(jnp.tanh(x), wup_ref[...])
    la = _dot(x, aup_ref[...])
    gate = _dot(_sigmoid(x), gup_ref[...])
    w_log = -_softplus(-(w0_ref[...] + lw)) - 0.5
    a = _sigmoid(a0_ref[...] + la)
    ones_bd = _head_ones()
    kkv = k * kk_ref[...]
    kk = kkv * lax.rsqrt(_head_sum(kkv * kkv, ones_bd) + EPS)
    k = k * (1.0 + (a - 1.0) * ka_ref[...])
    b = kk * a
    ld = -jnp.exp(w_log)

    m = _sbs_masks()
    same = m["same_w"]
    cum_incl = (_iota2((CHUNK, CHUNK), 0) >= _iota2((CHUNK, CHUNK), 1)).astype(jnp.bfloat16)
    chunks = range(ts // CHUNK)
    rows = [slice(ci * CHUNK, (ci + 1) * CHUNK) for ci in chunks]
    bd = lambda t: _bd(t, same)
    lo, hi = slice(0, CHUNK), slice(CHUNK, 2 * CHUNK)
    left, right = slice(0, STACK), slice(STACK, 2 * STACK)

    lg = [_dot_01x(cum_incl, ld[r_], 2) for r_ in rows]
    ll = [x[CHUNK - 1:CHUNK] for x in lg]
    e_neg = [jnp.exp(-x) for x in lg]
    e_rest = [jnp.exp(ll[i] - lg[i]) for i in chunks]
    kt = [kk[rows[i]] * jnp.exp(lg[i] - ld[rows[i]]) for i in chunks]
    rt = [r[rows[i]] * jnp.exp(lg[i]) for i in chunks]
    kh = [k[rows[i]] * e_neg[i] for i in chunks]
    bh = [b[rows[i]] * e_neg[i] for i in chunks]
    kbar = [k[rows[i]] * e_rest[i] for i in chunks]
    bbar = [b[rows[i]] * e_rest[i] for i in chunks]
    big = [_dot_nt(jnp.concatenate([kt[i], rt[i]], axis=0),
                   jnp.concatenate([bd(bh[i]), bd(kh[i])], axis=0)) for i in chunks]
    a_kb = [jnp.where(m["strict"], x[lo, left], 0.0) for x in big]
    a_kk = [jnp.where(m["strict"], x[lo, right], 0.0) for x in big]
    a_rb = [jnp.where(m["incl"], x[hi, left], 0.0) for x in big]
    a_rk = [jnp.where(m["incl"], x[hi, right], 0.0) for x in big]
    t = _inv_unit_lower(a_kb, m)
    akv = [_dot(jnp.concatenate([a_kk[i], a_rk[i]], axis=0), bd(v[rows[i]])) for i in chunks]
    tk = [_dot(t[i], jnp.concatenate([bd(akv[i][lo]), bd(kt[i])], axis=1)) for i in chunks]
    tb = [_dot_tn(tk[i], bbar[i]) for i in chunks]
    ub = [_diag_blocks(x[left]) for x in tb]
    wb = [bd(_diag_blocks(x[right])) for x in tb]
    vk = [_diag_blocks(_dot_tn(v[rows[i]], kbar[i])) for i in chunks]
    wr = [jnp.concatenate([tk[i][:, right], rt[i]], axis=0) for i in chunks]

    state = st_ref[...]
    ws, outs = [], []

    def finish(i):
        u = ws[i][lo] + tk[i][:, left]
        outs.append(ws[i][hi] + akv[i][hi] - _dot(a_rb[i], bd(u)))

    for i in chunks:
        prev = state
        state = state * jnp.exp(ll[i]) - _dot(state, wb[i]) + (vk[i] - ub[i])
        ws.append(_dot_nt(wr[i], bd(prev)))
        if i > 0:
            finish(i - 1)
    finish(chunks[-1])
    st_ref[...] = state
    y = jnp.concatenate(outs, axis=0)
    mean =_head_sum(y, ones_bd) * (1.0 / HEAD_DIM)
    yc = y - mean
    var = _head_sum(yc * yc, ones_bd) * (1.0 / HEAD_DIM)
    yn = yc * lax.rsqrt(var + RW_LN_EPS) * lng_ref[...] + lnb_ref[...]
    bonus = _head_sum(r * k * rk_ref[...], ones_bd) * v
    o_ref[...] = (yn + bonus) * gate


def _rw(p_rw, prm, l):
    B, S, n_rw = p_rw.shape
    ts = _tile(S, TS_REC)
    vec = lambda n=WIDTH: pl.BlockSpec((None, 1, n), lambda b, s: (l, 0, 0))
    lora = lambda: pl.BlockSpec((None, LANES, WIDTH), lambda b, s: (l, 0, 0))
    return pl.pallas_call(
        functools.partial(_rw_kernel, ts=ts),
        grid=(B, S // ts),
        in_specs=[
            pl.BlockSpec((None, ts, n_rw), lambda b, s: (b, s, 0)),
            vec(n_rw), vec(), lora(), vec(), lora(), lora(), vec(), vec(), vec(), vec(), vec(),
        ],
        out_specs=pl.BlockSpec((None, ts, WIDTH), lambda b, s: (b, s, 0)),
        out_shape=jax.ShapeDtypeStruct((B, S, WIDTH), F32),
        scratch_shapes=[pltpu.VMEM((SUBLANES, n_rw), F32), pltpu.VMEM((CHUNK, WIDTH), F32)],
        compiler_params=pltpu.CompilerParams(
            dimension_semantics=("parallel", "arbitrary"), vmem_limit_bytes=VMEM_LIMIT),
        name="rwkv7",
    )(p_rw, prm["rw_mu"], prm["rw_w0"], prm["rw_wup"], prm["rw_a0"], prm["rw_aup"], prm["rw_gup"],
      prm["rw_k_k"], prm["rw_k_a"], prm["rw_r_k"], prm["rw_ln_g"], prm["rw_ln_b"])


def _out_kernel(x_ref, mod_ref, g_ref, o0_ref, o1_ref, o2_ref, o3_ref,
                wg_ref, bg_ref, wb_ref, wo_ref, xo_ref, *, d):
    x = x_ref[...]
    mod = mod_ref[...]
    h = _modulated_norm(x, g_ref[...], mod[:, d:2 * d], mod[:, 0:d]).astype(MXU_DTYPE)
    mixed = None
    for n, o_ref in enumerate((o0_ref, o1_ref, o2_ref, o3_ref)):
        gate = _sigmoid(jnp.dot(h, wg_ref[n], preferred_element_type=F32) + bg_ref[n])
        y = jnp.dot(o_ref[...].astype(MXU_DTYPE), wb_ref[n], preferred_element_type=F32)
        mixed = gate * y if mixed is None else mixed + gate * y
    out = jnp.dot(mixed.astype(MXU_DTYPE), wo_ref[...], preferred_element_type=F32)
    xo_ref[...] = x + mod[:, 2 * d:3 * d] * out


def _out_proj(x, mod_l, norm_g, branches, prm, l):
    B, S, D = x.shape
    tm = _tile(S, TM_DENSE)
    tok = lambda n: pl.BlockSpec((None, tm, n), lambda b, s: (b, s, 0))
    const = lambda shape: pl.BlockSpec((None,) + shape, lambda b, s: (l,) + (0,) * len(shape),
                                       pipeline_mode=pl.Buffered(1))
    return pl.pallas_call(
        functools.partial(_out_kernel, d=D),
        grid=(B, S // tm),
        in_specs=[
            tok(D),
            pl.BlockSpec((None, 1, 6 * D), lambda b, s: (b, 0, 0)),
            const((1, D)),
            tok(WIDTH), tok(WIDTH), tok(WIDTH), tok(WIDTH),
            const((N_HEADS, D, D)), const((N_HEADS, 1, D)), const((N_HEADS, WIDTH, D)), const((D, D)),
        ],
        out_specs=tok(D),
        out_shape=jax.ShapeDtypeStruct((B, S, D), F32),
        compiler_params=pltpu.CompilerParams(
            dimension_semantics=("parallel", "parallel"), vmem_limit_bytes=VMEM_LIMIT),
        name="out_proj",
    )(x, mod_l, norm_g, *branches, prm["w_gate"], prm["b_gate"], prm["w_branch"], prm["w_out"])


def _ffn_kernel(x_ref, mod_ref, g_ref, wg_ref, wu_ref, cw_ref, wd_ref, fg_ref, xo_ref, halo_ref,
                *, d, tm, fc, final):
    @pl.when(pl.program_id(1) == 0)
    def _():
        halo_ref[...] = jnp.zeros_like(halo_ref)

    x = x_ref[...]
    mod = mod_ref[...]
    h = _modulated_norm(x, g_ref[...], mod[:, 4 * d:5 * d], mod[:, 3 * d:4 * d]).astype(MXU_DTYPE)
    acc = None
    for j in range(FFN_SPLIT):
        cols = slice(j * fc, (j + 1) * fc)
        a_pre = jnp.dot(h, wg_ref[:, cols], preferred_element_type=F32)
        up = jnp.dot(h, wu_ref[:, cols], preferred_element_type=F32)
        halo = halo_ref[:, cols]
        cw = cw_ref[:, cols]
        a = (cw[2:3] * a_pre + cw[1:2] * _shift_rows(a_pre, 1, halo)
             + cw[0:1] * _shift_rows(a_pre, 2, halo))
        halo_ref[:, cols] = a_pre[tm - SUBLANES:tm]
        act = (_gelu_tanh(a) * up).astype(MXU_DTYPE)
        part = jnp.dot(act, wd_ref[cols, :], preferred_element_type=F32)
        acc = part if acc is None else acc + part
    y = x + mod[:, 5 * d:6 * d] * acc
    if final:
        y = y * lax.rsqrt(jnp.mean(y * y, axis=-1, keepdims=True) + EPS) * fg_ref[...]
    xo_ref[...] = y


def _ffn(x, mod_l, norm_g, prm, final_g, l, final):
    B, S, D = x.shape
    F = prm["ffn_w_gate"].shape[-1]
    tm = _tile(S, TM_DENSE)
    fc = F // FFN_SPLIT
    tok = lambda n: pl.BlockSpec((None, tm, n), lambda b, s: (b, s, 0))
    const = lambda shape: pl.BlockSpec((None,) + shape, lambda b, s: (l,) + (0,) * len(shape),
                                       pipeline_mode=pl.Buffered(1))
    return pl.pallas_call(
        functools.partial(_ffn_kernel, d=D, tm=tm, fc=fc, final=final),
        grid=(B, S // tm),
        in_specs=[
            tok(D),
            pl.BlockSpec((None, 1, 6 * D), lambda b, s: (b, 0, 0)),
            const((1, D)),
            const((D, F)), const((D, F)), const((3, F)), const((F, D)),
            pl.BlockSpec((1, D), lambda b, s: (0, 0)),
        ],
        out_specs=tok(D),
        out_shape=jax.ShapeDtypeStruct((B, S, D), F32),
        scratch_shapes=[pltpu.VMEM((SUBLANES, F), F32)],
        compiler_params=pltpu.CompilerParams(
            dimension_semantics=("parallel", "arbitrary"), vmem_limit_bytes=VMEM_LIMIT),
        name="conv_ffn",
    )(x, mod_l, norm_g, prm["ffn_w_gate"], prm["ffn_w_up"], prm["ffn_conv_w"], prm["ffn_w_down"],
      final_g)


def _prepare(w_in, lru_conv_w, lru_conv_b, lru_w_r, lru_b_r, lru_w_i, lru_b_i, lru_lambda,
             gdn_conv_w, gdn_a_log, gdn_dt_bias, gdn_norm_g,
             rw_mu, rw_w0, rw_w_up, rw_a0, rw_a_up, rw_g_up, rw_k_k, rw_k_a, rw_r_k, rw_ln_g, rw_ln_b,
             w_branch, w_gate, b_gate, w_out, ffn_w_gate, ffn_w_up, ffn_conv_w, ffn_w_down):
    L, D, _ = w_in.shape
    bf = lambda t: t.astype(MXU_DTYPE)
    row = lambda t: t.reshape(L, 1, -1)

    n_ab = 2 * N_HEADS
    c_ab = 512 + 768 + 1024
    w_in_p = jnp.concatenate(
        [w_in[:, :, :c_ab], w_in[:, :, c_ab:c_ab + n_ab],
         jnp.zeros((L, D, LANES - n_ab), w_in.dtype), w_in[:, :, c_ab + n_ab:]], axis=-1)

    def block_diag(w):
        eye = jnp.eye(N_HEADS, dtype=w.dtype)
        return jnp.einsum("lnef,nm->lnemf", w, eye).reshape(L, WIDTH, WIDTH)

    def lane_pad(t, lo):
        return jnp.pad(t, ((0, 0), (lo, LANES - lo - t.shape[1]))).reshape(L, 1, LANES)

    def lora_pad(w, lo):
        return jnp.pad(w, ((0, 0), (lo, LANES - lo - w.shape[1]), (0, 0)))

    return dict(
        w_in=bf(w_in_p),
        lru_conv_w=lru_conv_w, lru_conv_b=row(lru_conv_b),
        lru_wr=bf(block_diag(lru_w_r)), lru_b_r=row(lru_b_r),
        lru_wi=bf(block_diag(lru_w_i)), lru_b_i=row(lru_b_i), lru_lambda=row(lru_lambda),
        gdn_conv_w=gdn_conv_w, gdn_a_log=lane_pad(gdn_a_log, 0), gdn_dt_bias=lane_pad(gdn_dt_bias, 0),
        gdn_norm_g=row(jnp.tile(gdn_norm_g, (1, N_HEADS))),
        rw_mu=row(rw_mu), rw_w0=row(rw_w0), rw_a0=row(rw_a0),
        rw_wup=bf(lora_pad(rw_w_up, 0)), rw_aup=bf(lora_pad(rw_a_up, W_LORA)),
        rw_gup=bf(lora_pad(rw_g_up, W_LORA + A_LORA)),
        rw_k_k=row(rw_k_k), rw_k_a=row(rw_k_a), rw_r_k=row(rw_r_k),
        rw_ln_g=row(rw_ln_g), rw_ln_b=row(rw_ln_b),
        w_gate=bf(w_gate), b_gate=b_gate.reshape(L, N_HEADS, 1, D), w_branch=bf(w_branch), w_out=bf(w_out),
        ffn_w_gate=bf(ffn_w_gate), ffn_w_up=bf(ffn_w_up), ffn_conv_w=ffn_conv_w, ffn_w_down=bf(ffn_w_down),
    )


def kernel(x, c, norm1_g, norm2_g, final_g, w_ada, b_ada, w_in, lru_conv_w, lru_conv_b, lru_w_r, lru_b_r, lru_w_i, lru_b_i, lru_lambda, gdn_conv_w, gdn_a_log, gdn_dt_bias, gdn_norm_g, rw_mu, rw_w0, rw_w_up, rw_a0, rw_a_up, rw_g_up, rw_k_k, rw_k_a, rw_r_k, rw_ln_g, rw_ln_b, w_branch, w_gate, b_gate, w_out, ffn_w_gate, ffn_w_up, ffn_conv_w, ffn_w_down):
    B, S, D = x.shape
    L = w_in.shape[0]
    prm = _prepare(w_in, lru_conv_w, lru_conv_b, lru_w_r, lru_b_r, lru_w_i, lru_b_i, lru_lambda,
                   gdn_conv_w, gdn_a_log, gdn_dt_bias, gdn_norm_g,
                   rw_mu, rw_w0, rw_w_up, rw_a0, rw_a_up, rw_g_up, rw_k_k, rw_k_a, rw_r_k,
                   rw_ln_g, rw_ln_b, w_branch, w_gate, b_gate, w_out,
                   ffn_w_gate, ffn_w_up, ffn_conv_w, ffn_w_down)
    mod = _modulation(c, w_ada, b_ada).reshape(L, B, 1, 6 * D)
    n1 = norm1_g.reshape(L, 1, D)
    n2 = norm2_g.reshape(L, 1, D)
    fg = final_g.reshape(1, D)
    for l in range(L):
        o_lru, p_sb, p_gdn, p_ab, p_rw = _in_proj(x, mod[l], n1, prm, l)
        branches = (o_lru, _sb(p_sb), _gdn(p_gdn, p_ab, prm, l), _rw(p_rw, prm, l))
        x = _out_proj(x, mod[l], n1, branches, prm, l)
        x = _ffn(x, mod[l], n2, prm, fg, l, final=(l == L - 1))
    return x
```
